```python
import math, functools
import jax, jax.numpy as jnp
from jax import lax
import numpy as np

D_MODEL = 2048
BATCH = 4
SEQ = 2048
DEPTH = 4
DEC_BATCH = 128
DEC_SEQ = 8
PAST_LEN = 8192
PAGE_SIZE = 128

MLA_HEADS = 8
MLA_Q_RANK = 512
MLA_KV_RANK = 256
MLA_NOPE = 128
MLA_ROPE = 64
MLA_V = 128
MLA_OUT = MLA_HEADS * MLA_V
MLA_Q_BLOCK = 128
ROPE_THETA = 10000.0
GDN_HEADS = 8
GDN_DK = 128
GDN_DV = 128
GDN_CONV = 4
GDN_CHUNK = 64
GDN_QKV = 2 * GDN_HEADS * GDN_DK + GDN_HEADS * GDN_DV
GDN_W = GDN_HEADS * GDN_DV
GMLP_GROUPS = 8
GMLP_GROUP_W = 128
GMLP_CHUNK = 128
GMLP_W = GMLP_GROUPS * GMLP_GROUP_W
N_BRANCH = 3
FFN_HIDDEN = -(-8 * D_MODEL // (3 * 256)) * 256
EPS = 1e-6
IN_SIZES = (MLA_Q_RANK, MLA_KV_RANK, MLA_ROPE, GDN_QKV, GDN_W, GDN_HEADS, GDN_HEADS, GMLP_W, GMLP_W, N_BRANCH * D_MODEL)
D_IN = sum(IN_SIZES)

kernel_name = 'hybrid_mla_gdn_gmlp_decode_step'


def rms_norm(x, g):
    xf = x.astype(jnp.float32)
    y = xf * lax.rsqrt(jnp.mean(xf * xf, axis=-1, keepdims=True) + EPS)
    return y.astype(x.dtype) * g


def layer_norm(x, g, b):
    xf = x.astype(jnp.float32)
    mu = jnp.mean(xf, axis=-1, keepdims=True)
    var = jnp.mean(jnp.square(xf - mu), axis=-1, keepdims=True)
    return ((xf - mu) * lax.rsqrt(var + EPS)).astype(x.dtype) * g + b


def l2_norm(x):
    xf = x.astype(jnp.float32)
    return (xf * lax.rsqrt(jnp.sum(xf * xf, axis=-1, keepdims=True) + EPS)).astype(x.dtype)


def split_cols(z):
    parts, start = [], 0
    for n in IN_SIZES:
        parts.append(z[..., start:start + n])
        start += n
    return parts


def rope_angles(pos):
    inv = ROPE_THETA ** (-jnp.arange(0, MLA_ROPE, 2, dtype=jnp.float32) / MLA_ROPE)
    ang = pos.astype(jnp.float32)[:, None] * inv[None, :]
    return jnp.cos(ang), jnp.sin(ang)


def apply_rope(x, cos, sin):
    xf = x.astype(jnp.float32)
    half = MLA_ROPE // 2
    x1, x2 = xf[..., :half], xf[..., half:]
    return jnp.concatenate([x1 * cos - x2 * sin, x2 * cos + x1 * sin], axis=-1).astype(x.dtype)


def mla_attend_prompt(qn, qr, ckv, kr, w_uk, w_uv, g_kn):
    B, S = qn.shape[:2]
    kn = rms_norm(jnp.einsum('bsr,rhd->bshd', ckv, w_uk), g_kn)
    v = jnp.einsum('bsr,rhd->bshd', ckv, w_uv)
    scale = (MLA_NOPE + MLA_ROPE) ** -0.5
    key_pos = jnp.arange(S)

    def block(i):
        q0 = i * MLA_Q_BLOCK
        qn_b = lax.dynamic_slice_in_dim(qn, q0, MLA_Q_BLOCK, axis=1)
        qr_b = lax.dynamic_slice_in_dim(qr, q0, MLA_Q_BLOCK, axis=1)
        s = jnp.einsum('bqhd,bkhd->bhqk', qn_b, kn) + jnp.einsum('bqhd,bkd->bhqk', qr_b, kr)
        causal = key_pos[None, :] <= (q0 + jnp.arange(MLA_Q_BLOCK))[:, None]
        s = jnp.where(causal, s.astype(jnp.float32) * scale, -jnp.inf)
        p = jax.nn.softmax(s, axis=-1).astype(v.dtype)
        return jnp.einsum('bhqk,bkhd->bqhd', p, v)

    o = lax.map(block, jnp.arange(S // MLA_Q_BLOCK))
    return jnp.moveaxis(o, 0, 1).reshape(B, S, MLA_OUT)


def mla_attend_sample(qn, qr, ckv_new, kr_new, w_uk, w_uv, g_kn, cache_ckv, cache_krope, page_table, layer):
    n_past = page_table.shape[1] * cache_ckv.shape[2]
    L = qn.shape[1]
    scale = (MLA_NOPE + MLA_ROPE) ** -0.5
    allowed = jnp.arange(n_past + L)[None, :] <= (n_past + jnp.arange(L))[:, None]

    def one(args):
        qn_b, qr_b, ckv_b, kr_b, pt = args
        ckv_all = jnp.concatenate([cache_ckv[layer, pt].reshape(n_past, MLA_KV_RANK), ckv_b], axis=0)
        kr_all = jnp.concatenate([cache_krope[layer, pt].reshape(n_past, MLA_ROPE), kr_b], axis=0)
        kn = rms_norm(jnp.einsum('tr,rhd->thd', ckv_all, w_uk), g_kn)
        s = jnp.einsum('lhd,thd->hlt', qn_b, kn) + jnp.einsum('lhd,td->hlt', qr_b, kr_all)
        s = jnp.where(allowed, s.astype(jnp.float32) * scale, -jnp.inf)
        p = jax.nn.softmax(s, axis=-1).astype(ckv_all.dtype)
        o_lat = jnp.einsum('hlt,tr->lhr', p, ckv_all)
        return jnp.einsum('lhr,rhd->lhd', o_lat, w_uv)

    o = lax.map(one, (qn, qr, ckv_new, kr_new, page_table))
    return o.reshape(o.shape[0], L, MLA_OUT)


def causal_conv(x, buf, w):
    L = x.shape[1]
    xp = jnp.concatenate([buf, x], axis=1)
    y = w[0] * xp[:, 0:L]
    for j in range(1, GDN_CONV):
        y = y + w[j] * xp[:, j:j + L]
    return y, xp[:, -(GDN_CONV - 1):]


def gated_delta_chunked(q, k, v, g, beta, s0):
    B, L, H, _ = q.shape
    C = min(GDN_CHUNK, L)
    pad = (-L) % C
    N = (L + pad) // C

    def prep(t):
        t = t.astype(jnp.float32)
        t = jnp.pad(t, [(0, 0), (0, pad)] + [(0, 0)] * (t.ndim - 2))
        t = t.reshape((B, N, C) + t.shape[2:])
        return jnp.moveaxis(t, (1, 3), (0, 2))

    qc, kc, vc, gc, bc = (prep(t) for t in (q, k, v, g, beta))
    gcum = jnp.cumsum(gc, axis=-1)
    incl = jnp.tril(jnp.ones((C, C), bool))
    strict = jnp.tril(jnp.ones((C, C), bool), -1)
    diff = gcum[..., :, None] - gcum[..., None, :]
    decay = jnp.where(incl, jnp.exp(jnp.where(incl, diff, 0.0)), 0.0)
    kb = kc * bc[..., None]
    a = jnp.where(strict, jnp.einsum('nbhid,nbhjd->nbhij', kb, kc) * decay, 0.0)
    rhs = jnp.concatenate([vc * bc[..., None], kb * jnp.exp(gcum)[..., None]], axis=-1)
    sol = lax.linalg.triangular_solve(a + jnp.eye(C, dtype=jnp.float32), rhs, left_side=True, lower=True, unit_diagonal=True)
    u, w = sol[..., :GDN_DV], sol[..., GDN_DV:]
    attn = jnp.where(incl, jnp.einsum('nbhid,nbhjd->nbhij', qc, kc) * decay, 0.0)
    g_last = gcum[..., -1]
    k_tail = kc * jnp.exp(g_last[..., None] - gcum)[..., None]
    q_head = qc * jnp.exp(gcum)[..., None]

    def step(S, xs):
        u_n, w_n, a_n, qh_n, kt_n, gl_n = xs
        v_new = u_n - jnp.einsum('bhcd,bhde->bhce', w_n, S)
        o = jnp.einsum('bhcd,bhde->bhce', qh_n, S) + jnp.einsum('bhij,bhje->bhie', a_n, v_new)
        S = S * jnp.exp(gl_n)[..., None, None] + jnp.einsum('bhcd,bhce->bhde', kt_n, v_new)
        return S, o

    S, o = lax.scan(step, s0.astype(jnp.float32), (u, w, attn, q_head, k_tail, g_last))
    o = jnp.moveaxis(o, (0, 2), (1, 3)).reshape(B, N * C, H, GDN_DV)[:, :L]
    return o.astype(v.dtype), S.astype(s0.dtype)


def gdn_branch(qkv_pre, z, a_logit, b_logit, conv0, s0, conv_w, a_log, dt_bias, g_onorm):
    B, L = qkv_pre.shape[:2]
    qkv, conv_new = causal_conv(qkv_pre, conv0, conv_w)
    qkv = jax.nn.silu(qkv)
    q, k, v = jnp.split(qkv, [GDN_HEADS * GDN_DK, 2 * GDN_HEADS * GDN_DK], axis=-1)
    q = l2_norm(q.reshape(B, L, GDN_HEADS, GDN_DK)) * (GDN_DK ** -0.5)
    k = l2_norm(k.reshape(B, L, GDN_HEADS, GDN_DK))
    v = v.reshape(B, L, GDN_HEADS, GDN_DV)
    beta = jax.nn.sigmoid(b_logit.astype(jnp.float32))
    g = -jnp.exp(a_log.astype(jnp.float32)) * jax.nn.softplus(a_logit.astype(jnp.float32) + dt_bias.astype(jnp.float32))
    o, s_new = gated_delta_chunked(q, k, v, g, beta, s0)
    o = rms_norm(o, g_onorm) * jax.nn.silu(z.reshape(B, L, GDN_HEADS, GDN_DV))
    return o.reshape(B, L, GDN_W), conv_new, s_new


def gmlp_branch(u, v, ln_g, ln_b, w_s, b_s):
    B, L = v.shape[:2]
    v = layer_norm(v, ln_g, ln_b)
    pad = (-L) % GMLP_CHUNK
    N = (L + pad) // GMLP_CHUNK
    vc = jnp.pad(v, ((0, 0), (0, pad), (0, 0))).reshape(B, N, GMLP_CHUNK, GMLP_GROUPS, GMLP_GROUP_W)
    w = jnp.where(jnp.tril(jnp.ones((GMLP_CHUNK, GMLP_CHUNK), bool)), w_s, jnp.zeros_like(w_s))
    s = jnp.einsum('gtj,bnjgc->bntgc', w, vc) + b_s.T[None, None, :, :, None]
    s = s.reshape(B, N * GMLP_CHUNK, GMLP_W)[:, :L]
    return u * s, v


def trunk_layer(x, cos, sin, attend, conv0, s0, p):
    B, L, _ = x.shape
    h = rms_norm(x, p['norm_mix_g'])
    hq, hkv, hkr, g_qkv, g_z, g_a, g_b, c_u, c_v, gate_logits = split_cols(h @ p['w_in'])
    q_lat = rms_norm(hq, p['mla_q_norm_g'])
    q = (q_lat @ p['mla_w_uq']).reshape(B, L, MLA_HEADS, MLA_NOPE + MLA_ROPE)
    qn = rms_norm(q[..., :MLA_NOPE], p['mla_qn_g'])
    qr = apply_rope(rms_norm(q[..., MLA_NOPE:], p['mla_qr_g']), cos[:, None], sin[:, None])
    ckv = rms_norm(hkv, p['mla_kv_norm_g'])
    kr = apply_rope(rms_norm(hkr, p['mla_kr_g']), cos, sin)
    o_mla = attend(qn, qr, ckv, kr, p['mla_w_uk'], p['mla_w_uv'], p['mla_kn_g'])
    o_gdn, conv_new, s_new = gdn_branch(g_qkv, g_z, g_a, g_b, conv0, s0, p['gdn_conv_w'], p['gdn_a_log'], p['gdn_dt_bias'], p['gdn_o_norm_g'])
    o_gmlp, v_rows = gmlp_branch(jax.nn.gelu(c_u, approximate=False), jax.nn.gelu(c_v, approximate=False), p['gmlp_ln_g'], p['gmlp_ln_b'], p['gmlp_w_s'], p['gmlp_b_s'])
    gates = jax.nn.sigmoid(gate_logits).reshape(B, L, N_BRANCH, D_MODEL)
    merged = (gates[:, :, 0] * (o_mla @ p['w_br_mla']) + gates[:, :, 1] * (o_gdn @ p['w_br_gdn']) + gates[:, :, 2] * (o_gmlp @ p['w_br_gmlp']))
    x = x + merged @ p['w_o']
    h2 = rms_norm(x, p['norm_ffn_g'])
    ff_gate, ff_up = jnp.split(h2 @ p['ffn_w_gu'], 2, axis=-1)
    x = x + (jax.nn.silu(ff_gate) * ff_up) @ p['ffn_w_down']
    return x, ckv, kr, s_new, conv_new, v_rows


def setup_inputs(seed: int = 0) -> dict:
    key = jax.random.key(seed)
    keys = list(jax.random.split(key, 48))
    f32 = jnp.float32

    def nk():
        return keys.pop()

    def nrm(shape, scale):
        return jax.random.normal(nk(), shape, f32) * scale

    def gain(n):
        return 1.0 + nrm((DEPTH, n), 0.02)

    n_pages = PAST_LEN // PAGE_SIZE
    n_used = DEC_BATCH * n_pages
    n_pool = n_used + max(1, n_used // 4)
    page_table = jax.random.permutation(nk(), n_pool)[:n_used].reshape(DEC_BATCH, n_pages).astype(jnp.int32)
    dt = jnp.exp(jax.random.uniform(nk(), (DEPTH, GDN_HEADS), f32, math.log(1e-3), math.log(1e-1)))
    return {
        'x_prompt': nrm((BATCH, SEQ, D_MODEL), 1.0),
        'x_sample': nrm((DEC_BATCH, DEC_SEQ, D_MODEL), 1.0),
        'cache_ckv': nrm((DEPTH, n_pool, PAGE_SIZE, MLA_KV_RANK), 1.0),
        'cache_krope': nrm((DEPTH, n_pool, PAGE_SIZE, MLA_ROPE), 1.0),
        'state_gdn': nrm((DEPTH, DEC_BATCH, GDN_HEADS, GDN_DK, GDN_DV), 0.1),
        'state_conv': nrm((DEPTH, DEC_BATCH, GDN_CONV - 1, GDN_QKV), 1.0),
        'page_table': page_table,
        'norm_mix_g': gain(D_MODEL),
        'w_in': nrm((DEPTH, D_MODEL, D_IN), D_MODEL ** -0.5),
        'mla_q_norm_g': gain(MLA_Q_RANK),
        'mla_w_uq': nrm((DEPTH, MLA_Q_RANK, MLA_HEADS * (MLA_NOPE + MLA_ROPE)), MLA_Q_RANK ** -0.5),
        'mla_qn_g': gain(MLA_NOPE),
        'mla_qr_g': gain(MLA_ROPE),
        'mla_kv_norm_g': gain(MLA_KV_RANK),
        'mla_kr_g': gain(MLA_ROPE),
        'mla_w_uk': nrm((DEPTH, MLA_KV_RANK, MLA_HEADS, MLA_NOPE), MLA_KV_RANK ** -0.5),
        'mla_kn_g': gain(MLA_NOPE),
        'mla_w_uv': nrm((DEPTH, MLA_KV_RANK, MLA_HEADS, MLA_V), MLA_KV_RANK ** -0.5),
        'gdn_conv_w': nrm((DEPTH, GDN_CONV, GDN_QKV), GDN_CONV ** -0.5),
        'gdn_a_log': jnp.log(jax.random.uniform(nk(), (DEPTH, GDN_HEADS), f32, 1.0, 16.0)),
        'gdn_dt_bias': dt + jnp.log(-jnp.expm1(-dt)),
        'gdn_o_norm_g': gain(GDN_DV),
        'gmlp_ln_g': gain(GMLP_W),
        'gmlp_ln_b': nrm((DEPTH, GMLP_W), 0.02),
        'gmlp_w_s': nrm((DEPTH, GMLP_GROUPS, GMLP_CHUNK, GMLP_CHUNK), GMLP_CHUNK ** -0.5),
        'gmlp_b_s': 1.0 + nrm((DEPTH, GMLP_GROUPS, GMLP_CHUNK), 0.02),
        'w_br_mla': nrm((DEPTH, MLA_OUT, D_MODEL), MLA_OUT ** -0.5),
        'w_br_gdn': nrm((DEPTH, GDN_W, D_MODEL), GDN_W ** -0.5),
        'w_br_gmlp': nrm((DEPTH, GMLP_W, D_MODEL), GMLP_W ** -0.5),
        'w_o': nrm((DEPTH, D_MODEL, D_MODEL), D_MODEL ** -0.5),
        'norm_ffn_g': gain(D_MODEL),
        'ffn_w_gu': nrm((DEPTH, D_MODEL, 2 * FFN_HIDDEN), D_MODEL ** -0.5),
        'ffn_w_down': nrm((DEPTH, FFN_HIDDEN, D_MODEL), FFN_HIDDEN ** -0.5),
    }


def reference(x_prompt, x_sample, cache_ckv, cache_krope, state_gdn, state_conv, page_table, norm_mix_g, w_in, mla_q_norm_g, mla_w_uq, mla_qn_g, mla_qr_g, mla_kv_norm_g, mla_kr_g, mla_w_uk, mla_kn_g, mla_w_uv, gdn_conv_w, gdn_a_log, gdn_dt_bias, gdn_o_norm_g, gmlp_ln_g, gmlp_ln_b, gmlp_w_s, gmlp_b_s, w_br_mla, w_br_gdn, w_br_gmlp, w_o, norm_ffn_g, ffn_w_gu, ffn_w_down):
    bp, seq = x_prompt.shape[:2]
    dec = x_sample.shape[1]
    n_past = page_table.shape[1] * cache_ckv.shape[2]
    cos_p, sin_p = rope_angles(jnp.arange(seq))
    cos_s, sin_s = rope_angles(n_past + jnp.arange(dec))
    conv0_p = jnp.zeros((bp, GDN_CONV - 1, GDN_QKV), x_prompt.dtype)
    s0_p = jnp.zeros((bp, GDN_HEADS, GDN_DK, GDN_DV), x_prompt.dtype)
    xp, xs = x_prompt, x_sample
    outs_p, outs_s = [], []
    for l in range(DEPTH):
        p = {
            'norm_mix_g': norm_mix_g[l], 'w_in': w_in[l],
            'mla_q_norm_g': mla_q_norm_g[l], 'mla_w_uq': mla_w_uq[l], 'mla_qn_g': mla_qn_g[l], 'mla_qr_g': mla_qr_g[l],
            'mla_kv_norm_g': mla_kv_norm_g[l], 'mla_kr_g': mla_kr_g[l], 'mla_w_uk': mla_w_uk[l], 'mla_kn_g': mla_kn_g[l], 'mla_w_uv': mla_w_uv[l],
            'gdn_conv_w': gdn_conv_w[l], 'gdn_a_log': gdn_a_log[l], 'gdn_dt_bias': gdn_dt_bias[l], 'gdn_o_norm_g': gdn_o_norm_g[l],
            'gmlp_ln_g': gmlp_ln_g[l], 'gmlp_ln_b': gmlp_ln_b[l], 'gmlp_w_s': gmlp_w_s[l], 'gmlp_b_s': gmlp_b_s[l],
            'w_br_mla': w_br_mla[l], 'w_br_gdn': w_br_gdn[l], 'w_br_gmlp': w_br_gmlp[l], 'w_o': w_o[l],
            'norm_ffn_g': norm_ffn_g[l], 'ffn_w_gu': ffn_w_gu[l], 'ffn_w_down': ffn_w_down[l],
        }
        attend_s = functools.partial(mla_attend_sample, cache_ckv=cache_ckv, cache_krope=cache_krope, page_table=page_table, layer=l)
        xp, *st_p = trunk_layer(xp, cos_p, sin_p, mla_attend_prompt, conv0_p, s0_p, p)
        xs, *st_s = trunk_layer(xs, cos_s, sin_s, attend_s, state_conv[l], state_gdn[l], p)
        outs_p.append(st_p)
        outs_s.append(st_s)

    def stack(outs, i):
        return jnp.stack([o[i] for o in outs], axis=0)

    return (xp, xs, stack(outs_p, 0), stack(outs_p, 1), stack(outs_p, 2), stack(outs_p, 3), stack(outs_s, 0), stack(outs_s, 1), stack(outs_s, 2), stack(outs_s, 3), stack(outs_s, 4))
```

```python
import functools
import math

import jax
import jax.numpy as jnp
from jax import lax
from jax.experimental import pallas as pl
from jax.experimental.pallas import tpu as pltpu

F32 = jnp.float32
BF16 = jnp.bfloat16

D_MODEL = 2048
MLA_HEADS = 8
MLA_Q_RANK = 512
MLA_KV_RANK = 256
MLA_NOPE = 128
MLA_ROPE = 64
MLA_V = 128
ROPE_THETA = 10000.0
GDN_HEADS = 8
GDN_DK = 128
GDN_DV = 128
GDN_CONV = 4
GDN_CHUNK = 64
GDN_QKV = 2 * GDN_HEADS * GDN_DK + GDN_HEADS * GDN_DV
GDN_W = GDN_HEADS * GDN_DV
GMLP_GROUPS = 8
GMLP_GROUP_W = 128
GMLP_CHUNK = 128
GMLP_W = GMLP_GROUPS * GMLP_GROUP_W
N_BRANCH = 3
FFN_HIDDEN = -(-8 * D_MODEL // (3 * 256)) * 256
EPS = 1e-6

LANES = 128
VMEM_LIMIT = 56 * 1024 * 1024

Z_GATE = 0
Z_QKV = Z_GATE + N_BRANCH * D_MODEL
Z_U = Z_QKV + GDN_QKV
Z_V = Z_U + GMLP_W
Z_Z = Z_V + GMLP_W
Z_QLAT = Z_Z + GDN_W
Z_KV = Z_QLAT + MLA_Q_RANK
Z_SEG = Z_KV + MLA_KV_RANK
Z_COLS = 13312
SEG_A = MLA_ROPE
SEG_B = MLA_ROPE + GDN_HEADS

NT = (((1,), (1,)), ((), ()))
TN = (((0,), (0,)), ((), ()))


def _pick(n, cands):
    for c in cands:
        if n % c == 0:
            return c
    raise ValueError(f"no block size in {cands} divides {n}")


def _params(*sem):
    return pltpu.CompilerParams(dimension_semantics=sem, vmem_limit_bytes=VMEM_LIMIT)


def _rms(x, n=None):
    n = x.shape[-1] if n is None else n
    return x * lax.rsqrt(jnp.sum(x * x, axis=-1, keepdims=True) * (1.0 / n) + EPS)


def _silu(x):
    return x * jax.nn.sigmoid(x)


def _gelu(x):
    return 0.5 * x * (1.0 + lax.erf(x * (1.0 / math.sqrt(2.0))))


def _rope128(y, cs, sg):
    lane = lax.broadcasted_iota(jnp.int32, y.shape, 1)
    swapped = jnp.where((lane & 63) < 32, pltpu.roll(y, 96, 1), pltpu.roll(y, 32, 1))
    return y * cs + swapped * sg


def _rmsnorm_kernel(x_ref, g_ref, o_ref):
    o_ref[...] = (_rms(x_ref[...]) * g_ref[...]).astype(o_ref.dtype)


def rmsnorm_cast(x, g):
    m, d = x.shape
    tm = _pick(m, (512, 256, 128))
    return pl.pallas_call(
        _rmsnorm_kernel,
        grid=(m // tm,),
        in_specs=[pl.BlockSpec((tm, d), lambda i: (i, 0)), pl.BlockSpec((1, d), lambda i: (0, 0))],
        out_specs=pl.BlockSpec((tm, d), lambda i: (i, 0)),
        out_shape=jax.ShapeDtypeStruct((m, d), BF16),
        compiler_params=_params("parallel"),
        name="rmsnorm_cast",
    )(x, g.reshape(1, d))


def _mm_kernel(a_ref, w_ref, o_ref):
    o_ref[...] = jnp.dot(a_ref[...].astype(BF16), w_ref[...], preferred_element_type=F32).astype(o_ref.dtype)


def _mm_res_kernel(a_ref, w_ref, x_ref, o_ref):
    o_ref[...] = x_ref[...] + jnp.dot(a_ref[...].astype(BF16), w_ref[...], preferred_element_type=F32)


def matmul(a, w, residual=None, tm_cands=(1024, 512, 256, 128), tn=512, name="matmul"):
    m, k = a.shape
    n = w.shape[1]
    tm = _pick(m, tm_cands)
    in_specs = [pl.BlockSpec((tm, k), lambda i, j: (i, 0)), pl.BlockSpec((k, tn), lambda i, j: (0, j))]
    args = [a, w]
    kern = _mm_kernel
    if residual is not None:
        in_specs.append(pl.BlockSpec((tm, tn), lambda i, j: (i, j)))
        args.append(residual)
        kern = _mm_res_kernel
    return pl.pallas_call(
        kern,
        grid=(m // tm, n // tn),
        in_specs=in_specs,
        out_specs=pl.BlockSpec((tm, tn), lambda i, j: (i, j)),
        out_shape=jax.ShapeDtypeStruct((m, n), F32),
        compiler_params=_params("parallel", "arbitrary"),
        name=name,
    )(*args)


def _swiglu_kernel(a_ref, wg_ref, wu_ref, o_ref):
    a = a_ref[...]
    gate = jnp.dot(a, wg_ref[...], preferred_element_type=F32)
    up = jnp.dot(a, wu_ref[...], preferred_element_type=F32)
    o_ref[...] = (_silu(gate) * up).astype(o_ref.dtype)


def swiglu_up(h, w_gu):
    m, k = h.shape
    hid = w_gu.shape[1] // 2
    tm = _pick(m, (1024, 512, 256, 128))
    tn = 512
    nb = hid // tn
    return pl.pallas_call(
        _swiglu_kernel,
        grid=(m // tm, nb),
        in_specs=[
            pl.BlockSpec((tm, k), lambda i, j: (i, 0)),
            pl.BlockSpec((k, tn), lambda i, j: (0, j)),
            pl.BlockSpec((k, tn), lambda i, j: (0, j + nb)),
        ],
        out_specs=pl.BlockSpec((tm, tn), lambda i, j: (i, j)),
        out_shape=jax.ShapeDtypeStruct((m, hid), BF16),
        compiler_params=_params("parallel", "arbitrary"),
        name="swiglu_up",
    )(h, w_gu, w_gu)


def _merge_kernel(a0_ref, a1_ref, a2_ref, w0_ref, w1_ref, w2_ref, g0_ref, g1_ref, g2_ref, o_ref):
    acc = None
    for a_ref, w_ref, g_ref in ((a0_ref, w0_ref, g0_ref), (a1_ref, w1_ref, g1_ref), (a2_ref, w2_ref, g2_ref)):
        y = jnp.dot(a_ref[...].astype(BF16), w_ref[...], preferred_element_type=F32)
        y = jax.nn.sigmoid(g_ref[...]) * y
        acc = y if acc is None else acc + y
    o_ref[...] = acc.astype(o_ref.dtype)


def gated_merge(o_mla, o_gdn, o_gmlp, w_mla, w_gdn, w_gmlp, z):
    m, k = o_mla.shape
    n = w_mla.shape[1]
    tm = _pick(m, (1024, 512, 256, 128))
    tn = 512
    nb = n // tn
    a_spec = pl.BlockSpec((tm, k), lambda i, j: (i, 0))
    w_spec = pl.BlockSpec((k, tn), lambda i, j: (0, j))
    g_specs = [pl.BlockSpec((tm, tn), functools.partial(lambda i, j, b: (i, Z_GATE // tn + b * nb + j), b=b)) for b in range(N_BRANCH)]
    return pl.pallas_call(
        _merge_kernel,
        grid=(m // tm, nb),
        in_specs=[a_spec, a_spec, a_spec, w_spec, w_spec, w_spec] + g_specs,
        out_specs=pl.BlockSpec((tm, tn), lambda i, j: (i, j)),
        out_shape=jax.ShapeDtypeStruct((m, n), BF16),
        compiler_params=_params("parallel", "arbitrary"),
        name="gated_merge",
    )(o_mla, o_gdn, o_gmlp, w_mla, w_gdn, w_gmlp, z, z, z)


def _mla_prep_kernel(ql_ref, kv_ref, seg_ref, cs_ref, sg_ref, gq_ref, gkv_ref, gkr_ref, qo_ref, ckv_ref, kr_ref):
    qo_ref[...] = (_rms(ql_ref[...]) * gq_ref[...]).astype(qo_ref.dtype)
    ckv_ref[...] = _rms(kv_ref[...]) * gkv_ref[...]
    seg = seg_ref[...]
    lane = lax.broadcasted_iota(jnp.int32, seg.shape, 1)
    x = jnp.where(lane < MLA_ROPE, seg, 0.0)
    kr_ref[...] = _rope128(_rms(x, MLA_ROPE) * gkr_ref[...], cs_ref[...], sg_ref[...])


def mla_prep(z, cs, sg, g_q, g_kv, g_kr_pad):
    m = z.shape[0]
    tm = _pick(m, (512, 256, 128))
    row = lambda w, c: pl.BlockSpec((tm, w), lambda i: (i, c))
    par = lambda w: pl.BlockSpec((1, w), lambda i: (0, 0))
    return pl.pallas_call(
        _mla_prep_kernel,
        grid=(m // tm,),
        in_specs=[
            row(MLA_Q_RANK, Z_QLAT // MLA_Q_RANK), row(MLA_KV_RANK, Z_KV // MLA_KV_RANK), row(LANES, Z_SEG // LANES),
            row(LANES, 0), row(LANES, 0), par(MLA_Q_RANK), par(MLA_KV_RANK), par(LANES),
        ],
        out_specs=[row(MLA_Q_RANK, 0), row(MLA_KV_RANK, 0), row(LANES, 0)],
        out_shape=[
            jax.ShapeDtypeStruct((m, MLA_Q_RANK), BF16),
            jax.ShapeDtypeStruct((m, MLA_KV_RANK), F32),
            jax.ShapeDtypeStruct((m, LANES), F32),
        ],
        compiler_params=_params("parallel"),
        name="mla_prep",
    )(z, z, z, cs, sg, g_q.reshape(1, -1), g_kv.reshape(1, -1), g_kr_pad)


def _q_kernel(a_ref, w_ref, cs_ref, sg_ref, gn_ref, gr_ref, o_ref, *, scale):
    a = a_ref[...]
    cs, sg = cs_ref[...], sg_ref[...]
    hw = MLA_NOPE + LANES
    for h in range(MLA_HEADS):
        qn = jnp.dot(a, w_ref[:, h * hw:h * hw + MLA_NOPE], preferred_element_type=F32)
        o_ref[:, h * hw:h * hw + MLA_NOPE] = (_rms(qn) * gn_ref[...] * scale).astype(o_ref.dtype)
        qr = jnp.dot(a, w_ref[:, h * hw + MLA_NOPE:(h + 1) * hw], preferred_element_type=F32)
        qr = _rope128(_rms(qr, MLA_ROPE) * gr_ref[...], cs, sg)
        o_ref[:, h * hw + MLA_NOPE:(h + 1) * hw] = (qr * scale).astype(o_ref.dtype)


def q_project(q_lat, w_uq_pad, cs, sg, g_qn, g_qr_pad):
    m, k = q_lat.shape
    n = w_uq_pad.shape[1]
    tm = _pick(m, (512, 256, 128))
    scale = (MLA_NOPE + MLA_ROPE) ** -0.5
    return pl.pallas_call(
        functools.partial(_q_kernel, scale=scale),
        grid=(m // tm,),
        in_specs=[
            pl.BlockSpec((tm, k), lambda i: (i, 0)), pl.BlockSpec((k, n), lambda i: (0, 0)),
            pl.BlockSpec((tm, LANES), lambda i: (i, 0)), pl.BlockSpec((tm, LANES), lambda i: (i, 0)),
            pl.BlockSpec((1, LANES), lambda i: (0, 0)), pl.BlockSpec((1, LANES), lambda i: (0, 0)),
        ],
        out_specs=pl.BlockSpec((tm, n), lambda i: (i, 0)),
        out_shape=jax.ShapeDtypeStruct((m, n), BF16),
        compiler_params=_params("parallel"),
        name="q_project",
    )(q_lat, w_uq_pad, cs, sg, g_qn.reshape(1, -1), g_qr_pad)


def _kv_kernel(a_ref, wk_ref, wv_ref, kr_ref, gk_ref, ko_ref, vo_ref):
    a = a_ref[...].astype(BF16)
    krb = kr_ref[...].astype(BF16)
    hw = MLA_NOPE + LANES
    for h in range(MLA_HEADS):
        kn = jnp.dot(a, wk_ref[:, h * MLA_NOPE:(h + 1) * MLA_NOPE], preferred_element_type=F32)
        ko_ref[:, h * hw:h * hw + MLA_NOPE] = (_rms(kn) * gk_ref[...]).astype(ko_ref.dtype)
        ko_ref[:, h * hw + MLA_NOPE:(h + 1) * hw] = krb
    vo_ref[...] = jnp.dot(a, wv_ref[...], preferred_element_type=F32).astype(vo_ref.dtype)


def kv_expand(ckv, kr_pad, w_uk, w_uv, g_kn, rows):
    tm = _pick(rows, (512, 256, 128))
    hw = MLA_NOPE + LANES
    return pl.pallas_call(
        _kv_kernel,
        grid=(rows // tm,),
        in_specs=[
            pl.BlockSpec((tm, MLA_KV_RANK), lambda i: (i, 0)),
            pl.BlockSpec(w_uk.shape, lambda i: (0, 0)), pl.BlockSpec(w_uv.shape, lambda i: (0, 0)),
            pl.BlockSpec((tm, LANES), lambda i: (i, 0)), pl.BlockSpec((1, MLA_NOPE), lambda i: (0, 0)),
        ],
        out_specs=[pl.BlockSpec((tm, MLA_HEADS * hw), lambda i: (i, 0)), pl.BlockSpec((tm, MLA_HEADS * MLA_V), lambda i: (i, 0))],
        out_shape=[jax.ShapeDtypeStruct((rows, MLA_HEADS * hw), BF16), jax.ShapeDtypeStruct((rows, MLA_HEADS * MLA_V), BF16)],
        compiler_params=_params("parallel"),
        name="kv_expand",
    )(ckv, w_uk, w_uv, kr_pad, g_kn.reshape(1, -1))


def _flash_kernel(q_ref, k_ref, v_ref, o_ref, *, tq):
    qi = pl.program_id(2)
    s = lax.dot_general(q_ref[...], k_ref[...], NT, preferred_element_type=F32)
    row = qi * tq + lax.broadcasted_iota(jnp.int32, s.shape, 0)
    col = lax.broadcasted_iota(jnp.int32, s.shape, 1)
    s = jnp.where(col <= row, s, -jnp.inf)
    p = jnp.exp(s - jnp.max(s, axis=-1, keepdims=True))
    l = jnp.sum(p, axis=-1, keepdims=True)
    o = jnp.dot(p.astype(BF16), v_ref[...], preferred_element_type=F32)
    o_ref[...] = (o / l).astype(o_ref.dtype)


def mla_prompt_attention(qcat, kcat, v, batch, seq):
    tq = _pick(seq, (256, 128))
    nq = seq // tq
    hw = MLA_NOPE + LANES
    return pl.pallas_call(
        functools.partial(_flash_kernel, tq=tq),
        grid=(batch, MLA_HEADS, nq),
        in_specs=[
            pl.BlockSpec((tq, hw), lambda b, h, i: (b * nq + i, h)),
            pl.BlockSpec((seq, hw), lambda b, h, i: (b, h)),
            pl.BlockSpec((seq, MLA_V), lambda b, h, i: (b, h)),
        ],
        out_specs=pl.BlockSpec((tq, MLA_V), lambda b, h, i: (b * nq + i, h)),
        out_shape=jax.ShapeDtypeStruct((batch * seq, MLA_HEADS * MLA_V), BF16),
        compiler_params=_params("parallel", "parallel", "arbitrary"),
        name="mla_prompt_attention",
    )(qcat, kcat, v)


def _absorb_kernel(q_ref, w_ref, g_ref, o_ref):
    q = q_ref[:, 0:MLA_NOPE].astype(F32) * g_ref[...]
    o_ref[0] = jnp.dot(q.astype(BF16), w_ref[0], preferred_element_type=F32)


def absorb_queries(qcat, w_ukt, g_kn, row0, rows):
    hw = MLA_NOPE + LANES
    return pl.pallas_call(
        _absorb_kernel,
        grid=(MLA_HEADS,),
        in_specs=[
            pl.BlockSpec((rows, hw), lambda h: (row0 // rows, h)),
            pl.BlockSpec((1, MLA_NOPE, MLA_KV_RANK), lambda h: (h, 0, 0)),
            pl.BlockSpec((1, MLA_NOPE), lambda h: (0, 0)),
        ],
        out_specs=pl.BlockSpec((1, rows, MLA_KV_RANK), lambda h: (h, 0, 0)),
        out_shape=jax.ShapeDtypeStruct((MLA_HEADS, rows, MLA_KV_RANK), F32),
        compiler_params=_params("parallel"),
        name="absorb_queries",
    )(qcat, w_ukt, g_kn.reshape(1, -1))


def _unabsorb_kernel(a_ref, w_ref, o_ref):
    o_ref[...] = jnp.dot(a_ref[0].astype(BF16), w_ref[0], preferred_element_type=F32)


def unabsorb_values(o_lat, w_uv_h):
    _, rows, r = o_lat.shape
    return pl.pallas_call(
        _unabsorb_kernel,
        grid=(MLA_HEADS,),
        in_specs=[pl.BlockSpec((1, rows, r), lambda h: (h, 0, 0)), pl.BlockSpec((1, r, MLA_V), lambda h: (h, 0, 0))],
        out_specs=pl.BlockSpec((rows, MLA_V), lambda h: (0, h)),
        out_shape=jax.ShapeDtypeStruct((rows, MLA_HEADS * MLA_V), F32),
        compiler_params=_params("parallel"),
        name="unabsorb_values",
    )(o_lat, w_uv_h)


def _sattn_kernel(pt_ref, qa_ref, qr_ref, cn_ref, kn_ref, wk_ref, ckv_hbm, kr_hbm, o_ref,
                  ckv_buf, kr_buf, sem, m_sc, l_sc, acc_sc, *, layer, pages_per_step, n_pages, page, dec):
    b = pl.program_id(0)
    c = pl.program_id(1)
    npc = pl.num_programs(1)
    step = b * npc + c
    nsteps = pl.num_programs(0) * npc
    slot = step % 2
    nq = qa_ref.shape[1]

    def copies(bb, cc, sl):
        out = []
        for g in range(pages_per_step):
            pg = pt_ref[bb * n_pages + cc * pages_per_step + g]
            out.append(pltpu.make_async_copy(ckv_hbm.at[layer, pg], ckv_buf.at[sl, pl.ds(g * page, page)], sem.at[sl, 0]))
            out.append(pltpu.make_async_copy(kr_hbm.at[layer, pg], kr_buf.at[sl, pl.ds(g * page, page)], sem.at[sl, 1]))
        return out

    @pl.when(step == 0)
    def _():
        for cp in copies(b, c, slot):
            cp.start()

    @pl.when(step + 1 < nsteps)
    def _():
        last = c + 1 == npc
        for cp in copies(jnp.where(last, b + 1, b), jnp.where(last, 0, c + 1), 1 - slot):
            cp.start()

    @pl.when(c == 0)
    def _():
        m_sc[...] = jnp.full(m_sc.shape, -jnp.inf, F32)
        l_sc[...] = jnp.zeros(l_sc.shape, F32)
        acc_sc[...] = jnp.zeros(acc_sc.shape, F32)

    qa = qa_ref[0].astype(BF16)
    qr = qr_ref[0].astype(BF16)
    wk = wk_ref[...]

    def scores(ck, kr):
        kn = lax.dot_general(wk, ck, NT, preferred_element_type=F32)
        inv = []
        for h in range(MLA_HEADS):
            kh = kn[h * MLA_NOPE:(h + 1) * MLA_NOPE]
            r = lax.rsqrt(jnp.sum(kh * kh, axis=0, keepdims=True) * (1.0 / MLA_NOPE) + EPS)
            inv.append(jnp.broadcast_to(r, (dec, r.shape[1])))
        inv = jnp.concatenate(inv, axis=0)
        s = lax.dot_general(qa, ck, NT, preferred_element_type=F32) * inv
        return s + lax.dot_general(qr, kr, NT, preferred_element_type=F32)

    def update(s, vals):
        m_prev = m_sc[...]
        m_new = jnp.maximum(m_prev, jnp.max(s, axis=-1, keepdims=True))
        alpha = jnp.exp(m_prev - m_new)
        p = jnp.exp(s - m_new)
        l_sc[...] = alpha * l_sc[...] + jnp.sum(p, axis=-1, keepdims=True)
        acc_sc[...] = alpha * acc_sc[...] + jnp.dot(p.astype(BF16), vals, preferred_element_type=F32)
        m_sc[...] = m_new

    for cp in copies(b, c, slot):
        cp.wait()
    ck = ckv_buf[slot].astype(BF16)
    update(scores(ck, kr_buf[slot].astype(BF16)), ck)

    @pl.when(c == npc - 1)
    def _():
        pad = page - dec
        cnew = jnp.concatenate([cn_ref[...], jnp.zeros((pad, cn_ref.shape[1]), F32)], axis=0).astype(BF16)
        knew = jnp.concatenate([kn_ref[:, 0:MLA_ROPE], jnp.zeros((pad, MLA_ROPE), F32)], axis=0).astype(BF16)
        s = scores(cnew, knew)
        qpos = lax.broadcasted_iota(jnp.int32, s.shape, 0) % dec
        kpos = lax.broadcasted_iota(jnp.int32, s.shape, 1)
        update(jnp.where(kpos <= qpos, s, -jnp.inf), cnew)
        o_ref[0] = acc_sc[...] / l_sc[...]


def mla_sample_attention(q_abs, q_rope, ckv, kr_pad, w_ukt2, cache_ckv, cache_krope, page_table, layer, row0, dec):
    nb, nq, r = q_abs.shape
    n_pages = page_table.shape[1]
    page = cache_ckv.shape[2]
    pps = _pick(n_pages, (8, 4, 2, 1))
    keys = pps * page
    grid_spec = pltpu.PrefetchScalarGridSpec(
        num_scalar_prefetch=1,
        grid=(nb, n_pages // pps),
        in_specs=[
            pl.BlockSpec((1, nq, r), lambda b, c, pt: (b, 0, 0)),
            pl.BlockSpec((1, nq, MLA_ROPE), lambda b, c, pt: (b, 0, 0)),
            pl.BlockSpec((dec, r), lambda b, c, pt: (row0 // dec + b, 0)),
            pl.BlockSpec((dec, LANES), lambda b, c, pt: (row0 // dec + b, 0)),
            pl.BlockSpec(w_ukt2.shape, lambda b, c, pt: (0, 0)),
            pl.BlockSpec(memory_space=pl.ANY),
            pl.BlockSpec(memory_space=pl.ANY),
        ],
        out_specs=pl.BlockSpec((1, nq, r), lambda b, c, pt: (b, 0, 0)),
        scratch_shapes=[
            pltpu.VMEM((2, keys, r), F32),
            pltpu.VMEM((2, keys, MLA_ROPE), F32),
            pltpu.SemaphoreType.DMA((2, 2)),
            pltpu.VMEM((nq, 1), F32),
            pltpu.VMEM((nq, 1), F32),
            pltpu.VMEM((nq, r), F32),
        ],
    )
    return pl.pallas_call(
        functools.partial(_sattn_kernel, layer=layer, pages_per_step=pps, n_pages=n_pages, page=page, dec=dec),
        grid_spec=grid_spec,
        out_shape=jax.ShapeDtypeStruct((nb, nq, r), F32),
        compiler_params=_params("arbitrary", "arbitrary"),
        name="mla_sample_attention",
    )(page_table.reshape(-1), q_abs, q_rope, ckv, kr_pad, w_ukt2, cache_ckv, cache_krope)


def _chunk_masks(c):
    ii = lax.broadcasted_iota(jnp.int32, (c, c), 0)
    jj = lax.broadcasted_iota(jnp.int32, (c, c), 1)
    return jj <= ii, jj < ii


def _gdn_prep_kernel(x_ref, seg_ref, c0_ref, cw_ref, al_ref, dt_ref, qkv_ref, gb_ref, cn_ref, a_ref, ext, *, rows, chunk):
    t = pl.program_id(1)
    hk = GDN_HEADS * GDN_DK
    tail = GDN_CONV - 1

    @pl.when(t == 0)
    def _():
        ext[8 - tail:8, :] = c0_ref[0]

    ext[8:8 + rows, :] = x_ref[...]
    cw = cw_ref[...]
    y = cw[0:1] * ext[5:5 + rows, :]
    for j in range(1, GDN_CONV):
        y = y + cw[j:j + 1] * ext[5 + j:5 + j + rows, :]
    new_tail = ext[8 + rows - tail:8 + rows, :]
    cn_ref[0] = new_tail
    ext[8 - tail:8, :] = new_tail
    y = _silu(y)

    seg = seg_ref[...]
    lane = lax.broadcasted_iota(jnp.int32, seg.shape, 1)
    beta = jax.nn.sigmoid(seg)
    g = -jnp.exp(al_ref[...]) * jax.nn.softplus(seg + dt_ref[...])
    ri = lax.broadcasted_iota(jnp.int32, (rows, rows), 0)
    ci = lax.broadcasted_iota(jnp.int32, (rows, rows), 1)
    cum = jnp.where((ri // chunk == ci // chunk) & (ci <= ri), 1.0, 0.0)
    gcum = jnp.dot(cum, g, precision=lax.Precision.HIGHEST, preferred_element_type=F32)
    gb = jnp.where((lane >= SEG_A) & (lane < SEG_B), gcum, beta)
    gb_ref[...] = gb
    gcum_t = gcum.T

    _, strict = _chunk_masks(chunk)
    qkv_ref[:, 2 * hk:] = y[:, 2 * hk:]
    for h in range(GDN_HEADS):
        qh = y[:, h * GDN_DK:(h + 1) * GDN_DK]
        qh = qh * lax.rsqrt(jnp.sum(qh * qh, axis=-1, keepdims=True) + EPS) * (GDN_DK ** -0.5)
        kh = y[:, hk + h * GDN_DK:hk + (h + 1) * GDN_DK]
        kh = kh * lax.rsqrt(jnp.sum(kh * kh, axis=-1, keepdims=True) + EPS)
        qkv_ref[:, h * GDN_DK:(h + 1) * GDN_DK] = qh
        qkv_ref[:, hk + h * GDN_DK:hk + (h + 1) * GDN_DK] = kh
        kb = kh * beta[:, SEG_B + h:SEG_B + h + 1]
        for c in range(rows // chunk):
            r0, r1 = c * chunk, (c + 1) * chunk
            gc = gcum[r0:r1, SEG_A + h:SEG_A + h + 1]
            gr = gcum_t[SEG_A + h:SEG_A + h + 1, r0:r1]
            decay = jnp.exp(jnp.where(strict, gc - gr, 0.0))
            kk = lax.dot_general(kb[r0:r1].astype(BF16), kh[r0:r1].astype(BF16), NT, preferred_element_type=F32)
            a_ref[c, h] = jnp.where(strict, kk * decay, 0.0)


def gdn_prep(z, conv0, conv_w, a_log_row, dt_row, row0, nseq, seqlen, rows, chunk):
    nt = seqlen // rows
    nc = rows // chunk
    base = row0 // rows
    return pl.pallas_call(
        functools.partial(_gdn_prep_kernel, rows=rows, chunk=chunk),
        grid=(nseq, nt),
        in_specs=[
            pl.BlockSpec((rows, GDN_QKV), lambda b, t: (base + b * nt + t, Z_QKV // GDN_QKV)),
            pl.BlockSpec((rows, LANES), lambda b, t: (base + b * nt + t, Z_SEG // LANES)),
            pl.BlockSpec((1, GDN_CONV - 1, GDN_QKV), lambda b, t: (b, 0, 0)),
            pl.BlockSpec((GDN_CONV, GDN_QKV), lambda b, t: (0, 0)),
            pl.BlockSpec((1, LANES), lambda b, t: (0, 0)),
            pl.BlockSpec((1, LANES), lambda b, t: (0, 0)),
        ],
        out_specs=[
            pl.BlockSpec((rows, GDN_QKV), lambda b, t: (b * nt + t, 0)),
            pl.BlockSpec((rows, LANES), lambda b, t: (b * nt + t, 0)),
            pl.BlockSpec((1, GDN_CONV - 1, GDN_QKV), lambda b, t: (b, 0, 0)),
            pl.BlockSpec((nc, GDN_HEADS, chunk, chunk), lambda b, t: (b * nt + t, 0, 0, 0)),
        ],
        out_shape=[
            jax.ShapeDtypeStruct((nseq * seqlen, GDN_QKV), F32),
            jax.ShapeDtypeStruct((nseq * seqlen, LANES), F32),
            jax.ShapeDtypeStruct((nseq, GDN_CONV - 1, GDN_QKV), F32),
            jax.ShapeDtypeStruct((nseq * seqlen // chunk, GDN_HEADS, chunk, chunk), F32),
        ],
        scratch_shapes=[pltpu.VMEM((rows + 8, GDN_QKV), F32)],
        compiler_params=_params("parallel", "arbitrary"),
        name="gdn_prep",
    )(z, z, conv0, conv_w, a_log_row, dt_row)


def _trinv_kernel(a_ref, o_ref, t_sc, *, n, kb):
    i = pl.program_id(0)
    for b in range(n // kb):
        k0 = b * kb

        def body(j, acc, k0=k0):
            return acc + a_ref[0, j][None] * t_sc[j, k0:k0 + kb]

        acc = lax.fori_loop(k0, i, body, jnp.zeros((kb,) + t_sc.shape[2:], F32))
        kidx = k0 + lax.broadcasted_iota(jnp.int32, acc.shape, 0)
        row = jnp.where(kidx == i, 1.0, 0.0) - acc
        t_sc[i, k0:k0 + kb] = row
        o_ref[0, k0:k0 + kb] = row


def unit_lower_inverse(a):
    nsys, n, _ = a.shape
    tile = 8 * LANES
    npad = -(-nsys // tile) * tile
    at = jnp.transpose(a, (1, 2, 0))
    if npad != nsys:
        at = jnp.pad(at, ((0, 0), (0, 0), (0, npad - nsys)))
    at = at.reshape(n, n, npad // LANES, LANES)
    kb = min(n, 16)
    outs = []
    for s in range(npad // tile):
        blk = at[:, :, s * 8:(s + 1) * 8]
        outs.append(pl.pallas_call(
            functools.partial(_trinv_kernel, n=n, kb=kb),
            grid=(n,),
            in_specs=[pl.BlockSpec((1, n, 8, LANES), lambda i: (i, 0, 0, 0))],
            out_specs=pl.BlockSpec((1, n, 8, LANES), lambda i: (i, 0, 0, 0)),
            out_shape=jax.ShapeDtypeStruct((n, n, 8, LANES), F32),
            scratch_shapes=[pltpu.VMEM((n, n, 8, LANES), F32)],
            compiler_params=_params("arbitrary"),
            name="unit_lower_inverse",
        )(blk))
    t = outs[0] if len(outs) == 1 else jnp.concatenate(outs, axis=2)
    t = t.reshape(n, n, npad)[:, :, :nsys]
    return jnp.transpose(t, (2, 0, 1))


def _gdn_scan_kernel(qkv_ref, gb_ref, z_ref, ti_ref, s0_ref, gon_ref, o_ref, so_ref, s_sc, *, rows, chunk):
    t = pl.program_id(1)
    hk = GDN_HEADS * GDN_DK

    @pl.when(t == 0)
    def _():
        s_sc[...] = s0_ref[0]

    gb = gb_ref[...]
    gb_t = gb.T
    incl, _ = _chunk_masks(chunk)
    for c in range(rows // chunk):
        r0, r1 = c * chunk, (c + 1) * chunk
        for h in range(GDN_HEADS):
            gc = gb[r0:r1, SEG_A + h:SEG_A + h + 1]
            gr = gb_t[SEG_A + h:SEG_A + h + 1, r0:r1]
            beta = gb[r0:r1, SEG_B + h:SEG_B + h + 1]
            decay = jnp.exp(jnp.where(incl, gc - gr, 0.0))
            q = qkv_ref[r0:r1, h * GDN_DK:(h + 1) * GDN_DK]
            k = qkv_ref[r0:r1, hk + h * GDN_DK:hk + (h + 1) * GDN_DK]
            v = qkv_ref[r0:r1, 2 * hk + h * GDN_DV:2 * hk + (h + 1) * GDN_DV]
            attn = lax.dot_general(q.astype(BF16), k.astype(BF16), NT, preferred_element_type=F32)
            attn = jnp.where(incl, attn * decay, 0.0)
            egc = jnp.exp(gc)
            glast = gc[chunk - 1:chunk, :]
            rhs = jnp.concatenate([v * beta, k * (beta * egc)], axis=1)
            sol = jnp.dot(ti_ref[c, h], rhs, precision=lax.Precision.HIGHEST, preferred_element_type=F32)
            u, w = sol[:, :GDN_DV], sol[:, GDN_DV:]
            s_h = s_sc[h]
            wq = jnp.concatenate([w, q * egc], axis=0).astype(BF16)
            ws = jnp.dot(wq, s_h.astype(BF16), preferred_element_type=F32)
            v_new = u - ws[:chunk]
            v_new_b = v_new.astype(BF16)
            o = ws[chunk:] + jnp.dot(attn.astype(BF16), v_new_b, preferred_element_type=F32)
            k_tail = (k * jnp.exp(glast - gc)).astype(BF16)
            s_sc[h] = s_h * jnp.exp(glast) + lax.dot_general(k_tail, v_new_b, TN, preferred_element_type=F32)
            zg = z_ref[r0:r1, h * GDN_DV:(h + 1) * GDN_DV]
            o_ref[r0:r1, h * GDN_DV:(h + 1) * GDN_DV] = _rms(o) * gon_ref[...] * _silu(zg)

    @pl.when(t == pl.num_programs(1) - 1)
    def _():
        so_ref[0] = s_sc[...]


def gdn_scan(qkv, gb, z, tinv, s0, g_onorm, row0, nseq, seqlen, rows, chunk):
    nt = seqlen // rows
    nc = rows // chunk
    base = row0 // rows
    return pl.pallas_call(
        functools.partial(_gdn_scan_kernel, rows=rows, chunk=chunk),
        grid=(nseq, nt),
        in_specs=[
            pl.BlockSpec((rows, GDN_QKV), lambda b, t: (b * nt + t, 0)),
            pl.BlockSpec((rows, LANES), lambda b, t: (b * nt + t, 0)),
            pl.BlockSpec((rows, GDN_W), lambda b, t: (base + b * nt + t, Z_Z // GDN_W)),
            pl.BlockSpec((nc, GDN_HEADS, chunk, chunk), lambda b, t: (b * nt + t, 0, 0, 0)),
            pl.BlockSpec((1, GDN_HEADS, GDN_DK, GDN_DV), lambda b, t: (b, 0, 0, 0)),
            pl.BlockSpec((1, GDN_DV), lambda b, t: (0, 0)),
        ],
        out_specs=[
            pl.BlockSpec((rows, GDN_W), lambda b, t: (b * nt + t, 0)),
            pl.BlockSpec((1, GDN_HEADS, GDN_DK, GDN_DV), lambda b, t: (b, 0, 0, 0)),
        ],
        out_shape=[
            jax.ShapeDtypeStruct((nseq * seqlen, GDN_W), F32),
            jax.ShapeDtypeStruct((nseq, GDN_HEADS, GDN_DK, GDN_DV), F32),
        ],
        scratch_shapes=[pltpu.VMEM((GDN_HEADS, GDN_DK, GDN_DV), F32)],
        compiler_params=_params("parallel", "arbitrary"),
        name="gdn_scan",
    )(qkv, gb, z, tinv, s0, g_onorm.reshape(1, -1))


def gdn_branch(z, conv0, s0, conv_w, a_log_row, dt_row, g_onorm, row0, nseq, seqlen):
    chunk = min(GDN_CHUNK, seqlen)
    rows = _pick(seqlen, (256, 128, 64, 8))
    qkv, gb, conv_new, a = gdn_prep(z, conv0, conv_w, a_log_row, dt_row, row0, nseq, seqlen, rows, chunk)
    tinv = unit_lower_inverse(a.reshape(-1, chunk, chunk)).reshape(a.shape)
    o, s_new = gdn_scan(qkv, gb, z, tinv, s0, g_onorm, row0, nseq, seqlen, rows, chunk)
    return o, conv_new, s_new


def _layer_norm(v, g, b):
    mu = jnp.mean(v, axis=-1, keepdims=True)
    d = v - mu
    return d * lax.rsqrt(jnp.mean(d * d, axis=-1, keepdims=True) + EPS) * g + b


def _gmlp_prompt_kernel(u_ref, v_ref, lg_ref, lb_ref, w_ref, bias_ref, o_ref, *, rows):
    vn = _layer_norm(_gelu(v_ref[...]), lg_ref[...], lb_ref[...]).astype(BF16)
    u = _gelu(u_ref[...])
    tril, _ = _chunk_masks(GMLP_CHUNK)
    bias = bias_ref[...]
    for g in range(GMLP_GROUPS):
        c0, c1 = g * GMLP_GROUP_W, (g + 1) * GMLP_GROUP_W
        wg = jnp.where(tril, w_ref[g], 0.0).astype(BF16)
        for c in range(rows // GMLP_CHUNK):
            r0, r1 = c * GMLP_CHUNK, (c + 1) * GMLP_CHUNK
            s = jnp.dot(wg, vn[r0:r1, c0:c1], preferred_element_type=F32) + bias[:, c0:c1]
            o_ref[r0:r1, c0:c1] = (u[r0:r1, c0:c1] * s).astype(o_ref.dtype)


def gmlp_prompt(z, ln_g, ln_b, w_s, bias_full, nrows):
    rows = _pick(nrows, (512, 256, 128))
    row = lambda cb: pl.BlockSpec((rows, GMLP_W), lambda i: (i, cb))
    par = lambda: pl.BlockSpec((1, GMLP_W), lambda i: (0, 0))
    return pl.pallas_call(
        functools.partial(_gmlp_prompt_kernel, rows=rows),
        grid=(nrows // rows,),
        in_specs=[
            row(Z_U // GMLP_W), row(Z_V // GMLP_W), par(), par(),
            pl.BlockSpec(w_s.shape, lambda i: (0, 0, 0)),
            pl.BlockSpec(bias_full.shape, lambda i: (0, 0)),
        ],
        out_specs=row(0),
        out_shape=jax.ShapeDtypeStruct((nrows, GMLP_W), BF16),
        compiler_params=_params("parallel"),
        name="gmlp_prompt",
    )(z, z, ln_g.reshape(1, -1), ln_b.reshape(1, -1), w_s, bias_full)


def _gmlp_sample_kernel(u_ref, v_ref, lg_ref, lb_ref, wt_ref, b_ref, o_ref, vr_ref, *, rows, dec):
    vn = _layer_norm(_gelu(v_ref[...]), lg_ref[...], lb_ref[...])
    vr_ref[...] = vn
    v3 = vn.reshape(rows // dec, dec, GMLP_W)
    tpos = lax.broadcasted_iota(jnp.int32, (dec, GMLP_W), 0)
    s3 = jnp.broadcast_to(b_ref[...][None], v3.shape)
    for j in range(dec):
        wj = jnp.where(tpos >= j, wt_ref[j], 0.0)
        s3 = s3 + wj[None] * v3[:, j:j + 1, :]
    u3 = _gelu(u_ref[...]).reshape(rows // dec, dec, GMLP_W)
    o_ref[...] = (u3 * s3).reshape(rows, GMLP_W)


def gmlp_sample(z, ln_g, ln_b, wt, b_dec, row0, nrows, dec):
    rows = _pick(nrows, (256, 128))
    base = row0 // rows
    row = lambda cb, off: pl.BlockSpec((rows, GMLP_W), lambda i: (off + i, cb))
    par = lambda: pl.BlockSpec((1, GMLP_W), lambda i: (0, 0))
    return pl.pallas_call(
        functools.partial(_gmlp_sample_kernel, rows=rows, dec=dec),
        grid=(nrows // rows,),
        in_specs=[
            row(Z_U // GMLP_W, base), row(Z_V // GMLP_W, base), par(), par(),
            pl.BlockSpec(wt.shape, lambda i: (0, 0, 0)),
            pl.BlockSpec(b_dec.shape, lambda i: (0, 0)),
        ],
        out_specs=[row(0, 0), row(0, 0)],
        out_shape=[jax.ShapeDtypeStruct((nrows, GMLP_W), F32), jax.ShapeDtypeStruct((nrows, GMLP_W), F32)],
        compiler_params=_params("parallel"),
        name="gmlp_sample",
    )(z, z, ln_g.reshape(1, -1), ln_b.reshape(1, -1), wt, b_dec)


def _rope_tables(pos):
    inv = ROPE_THETA ** (-jnp.arange(0, MLA_ROPE, 2, dtype=F32) / MLA_ROPE)
    ang = pos.astype(F32)[:, None] * inv[None, :]
    cos, sin = jnp.cos(ang), jnp.sin(ang)
    zero = jnp.zeros((pos.shape[0], LANES - MLA_ROPE), F32)
    return jnp.concatenate([cos, cos, zero], axis=1), jnp.concatenate([-sin, sin, zero], axis=1)


def _pad_lanes(v, offset=0):
    return jnp.zeros((1, LANES), F32).at[0, offset:offset + v.shape[0]].set(v)


def kernel(x_prompt, x_sample, cache_ckv, cache_krope, state_gdn, state_conv, page_table, norm_mix_g, w_in, mla_q_norm_g, mla_w_uq, mla_qn_g, mla_qr_g, mla_kv_norm_g, mla_kr_g, mla_w_uk, mla_kn_g, mla_w_uv, gdn_conv_w, gdn_a_log, gdn_dt_bias, gdn_o_norm_g, gmlp_ln_g, gmlp_ln_b, gmlp_w_s, gmlp_b_s, w_br_mla, w_br_gdn, w_br_gmlp, w_o, norm_ffn_g, ffn_w_gu, ffn_w_down):
    bp, seq, d = x_prompt.shape
    bs, dec, _ = x_sample.shape
    depth = w_in.shape[0]
    mp, ms = bp * seq, bs * dec
    n_past = page_table.shape[1] * cache_ckv.shape[2]
    assert d == D_MODEL and dec == 8 and mp % ms == 0 and seq % GMLP_CHUNK == 0 and seq % GDN_CHUNK == 0

    x = jnp.concatenate([x_prompt.reshape(mp, d), x_sample.reshape(ms, d)], axis=0)
    pos = jnp.concatenate([jnp.tile(jnp.arange(seq), bp), jnp.tile(n_past + jnp.arange(dec), bs)])
    cs, sg = _rope_tables(pos)
    conv0_p = jnp.zeros((bp, GDN_CONV - 1, GDN_QKV), F32)
    s0_p = jnp.zeros((bp, GDN_HEADS, GDN_DK, GDN_DV), F32)

    outs = [[] for _ in range(9)]
    for l in range(depth):
        wi = w_in[l]
        o = [0]
        for n in (MLA_Q_RANK, MLA_KV_RANK, MLA_ROPE, GDN_QKV, GDN_W, GDN_HEADS, GDN_HEADS, GMLP_W, GMLP_W, N_BRANCH * D_MODEL):
            o.append(o[-1] + n)
        w_in_p = jnp.concatenate([
            wi[:, o[9]:o[10]], wi[:, o[3]:o[4]], wi[:, o[7]:o[8]], wi[:, o[8]:o[9]], wi[:, o[4]:o[5]],
            wi[:, o[0]:o[1]], wi[:, o[1]:o[2]], wi[:, o[2]:o[3]], wi[:, o[5]:o[7]],
            jnp.zeros((d, Z_COLS - Z_SEG - MLA_ROPE - 2 * GDN_HEADS), F32),
        ], axis=1).astype(BF16)
        w_uq = mla_w_uq[l].reshape(MLA_Q_RANK, MLA_HEADS, MLA_NOPE + MLA_ROPE)
        w_uq = jnp.concatenate([w_uq, jnp.zeros((MLA_Q_RANK, MLA_HEADS, LANES - MLA_ROPE), F32)], axis=-1)
        w_uq = w_uq.reshape(MLA_Q_RANK, -1).astype(BF16)
        w_uk = mla_w_uk[l].reshape(MLA_KV_RANK, -1).astype(BF16)
        w_uv = mla_w_uv[l].reshape(MLA_KV_RANK, -1).astype(BF16)
        w_ukt = jnp.transpose(mla_w_uk[l], (1, 2, 0)).astype(BF16)
        w_uv_h = jnp.transpose(mla_w_uv[l], (1, 0, 2)).astype(BF16)
        g_kr_pad = _pad_lanes(mla_kr_g[l])
        g_qr_pad = _pad_lanes(mla_qr_g[l])
        a_log_row = _pad_lanes(gdn_a_log[l], SEG_A)
        dt_row = _pad_lanes(gdn_dt_bias[l], SEG_A)
        bias_full = jnp.repeat(gmlp_b_s[l].T, GMLP_GROUP_W, axis=1)
        wt_dec = jnp.repeat(jnp.transpose(gmlp_w_s[l][:, :dec, :dec], (2, 1, 0)), GMLP_GROUP_W, axis=2)

        h = rmsnorm_cast(x, norm_mix_g[l])
        z = matmul(h, w_in_p, name="in_proj")

        q_lat, ckv, kr_pad = mla_prep(z, cs, sg, mla_q_norm_g[l], mla_kv_norm_g[l], g_kr_pad)
        qcat = q_project(q_lat, w_uq, cs, sg, mla_qn_g[l], g_qr_pad)
        kcat, vexp = kv_expand(ckv, kr_pad, w_uk, w_uv, mla_kn_g[l], mp)
        o_mla_p = mla_prompt_attention(qcat, kcat, vexp, bp, seq)
        q_abs = absorb_queries(qcat, w_ukt, mla_kn_g[l], mp, ms)
        q_abs = jnp.transpose(q_abs.reshape(MLA_HEADS, bs, dec, MLA_KV_RANK), (1, 0, 2, 3)).reshape(bs, MLA_HEADS * dec, MLA_KV_RANK)
        q_rope = qcat[mp:].reshape(bs, dec, MLA_HEADS, MLA_NOPE + LANES)[..., MLA_NOPE:MLA_NOPE + MLA_ROPE]
        q_rope = jnp.transpose(q_rope, (0, 2, 1, 3)).reshape(bs, MLA_HEADS * dec, MLA_ROPE).astype(F32)
        o_lat = mla_sample_attention(q_abs, q_rope, ckv, kr_pad, w_ukt.reshape(-1, MLA_KV_RANK), cache_ckv, cache_krope, page_table, l, mp, dec)
        o_lat = jnp.transpose(o_lat.reshape(bs, MLA_HEADS, dec, MLA_KV_RANK), (1, 0, 2, 3)).reshape(MLA_HEADS, ms, MLA_KV_RANK)
        o_mla_s = unabsorb_values(o_lat, w_uv_h)
        o_mla = jnp.concatenate([o_mla_p, o_mla_s.astype(BF16)], axis=0)

        o_gdn_p, conv_p, s_p = gdn_branch(z, conv0_p, s0_p, gdn_conv_w[l], a_log_row, dt_row, gdn_o_norm_g[l], 0, bp, seq)
        o_gdn_s, conv_s, s_s = gdn_branch(z, state_conv[l], state_gdn[l], gdn_conv_w[l], a_log_row, dt_row, gdn_o_norm_g[l], mp, bs, dec)
        o_gdn = jnp.concatenate([o_gdn_p, o_gdn_s], axis=0)

        o_gmlp_p = gmlp_prompt(z, gmlp_ln_g[l], gmlp_ln_b[l], gmlp_w_s[l], bias_full, mp)
        o_gmlp_s, v_rows = gmlp_sample(z, gmlp_ln_g[l], gmlp_ln_b[l], wt_dec, bias_full[:dec], mp, ms, dec)
        o_gmlp = jnp.concatenate([o_gmlp_p, o_gmlp_s.astype(BF16)], axis=0)

        merged = gated_merge(o_mla, o_gdn, o_gmlp, w_br_mla[l].astype(BF16), w_br_gdn[l].astype(BF16), w_br_gmlp[l].astype(BF16), z)
        x = matmul(merged, w_o[l].astype(BF16), residual=x, name="out_proj")
        h2 = rmsnorm_cast(x, norm_ffn_g[l])
        act = swiglu_up(h2, ffn_w_gu[l].astype(BF16))
        x = matmul(act, ffn_w_down[l].astype(BF16), residual=x, tm_cands=(512, 256, 128), name="ffn_down")

        kr = kr_pad[:, :MLA_ROPE]
        for i, v in enumerate((
            ckv[:mp].reshape(bp, seq, -1), kr[:mp].reshape(bp, seq, -1), s_p, conv_p,
            ckv[mp:].reshape(bs, dec, -1), kr[mp:].reshape(bs, dec, -1), s_s, conv_s, v_rows.reshape(bs, dec, -1),
        )):
            outs[i].append(v)

    return (x[:mp].reshape(bp, seq, d), x[mp:].reshape(bs, dec, d)) + tuple(jnp.stack(v, axis=0) for v in outs)
```

```python
import functools
import math

import jax
import jax.numpy as jnp
from jax import lax
from jax.experimental import pallas as pl
from jax.experimental.pallas import tpu as pltpu

F32 = jnp.float32
BF16 = jnp.bfloat16

D_MODEL = 2048
MLA_HEADS = 8
MLA_Q_RANK = 512
MLA_KV_RANK = 256
MLA_NOPE = 128
MLA_ROPE = 64
MLA_V = 128
ROPE_THETA = 10000.0
GDN_HEADS = 8
GDN_DK = 128
GDN_DV = 128
GDN_CONV = 4
GDN_CHUNK = 64
GDN_QKV = 2 * GDN_HEADS * GDN_DK + GDN_HEADS * GDN_DV
GDN_W = GDN_HEADS * GDN_DV
GMLP_GROUPS = 8
GMLP_GROUP_W = 128
GMLP_CHUNK = 128
GMLP_W = GMLP_GROUPS * GMLP_GROUP_W
N_BRANCH = 3
FFN_HIDDEN = -(-8 * D_MODEL // (3 * 256)) * 256
EPS = 1e-6

LANES = 128
VMEM_LIMIT = 56 * 1024 * 1024

Z_GATE = 0
Z_QKV = Z_GATE + N_BRANCH * D_MODEL
Z_U = Z_QKV + GDN_QKV
Z_V = Z_U + GMLP_W
Z_Z = Z_V + GMLP_W
Z_QLAT = Z_Z + GDN_W
Z_KV = Z_QLAT + MLA_Q_RANK
Z_SEG = Z_KV + MLA_KV_RANK
Z_COLS = 13312
SEG_A = MLA_ROPE
SEG_B = MLA_ROPE + GDN_HEADS

NT = (((1,), (1,)), ((), ()))
TN = (((0,), (0,)), ((), ()))


def _pick(n, cands):
    for c in cands:
        if n % c == 0:
            return c
    raise ValueError(f"no block size in {cands} divides {n}")


def _params(*sem):
    return pltpu.CompilerParams(dimension_semantics=sem, vmem_limit_bytes=VMEM_LIMIT)


def _rms(x, n=None):
    n = x.shape[-1] if n is None else n
    return x * lax.rsqrt(jnp.sum(x * x, axis=-1, keepdims=True) * (1.0 / n) + EPS)


def _silu(x):
    return x * jax.nn.sigmoid(x)


def _gelu(x):
    return 0.5 * x * (1.0 + lax.erf(x * (1.0 / math.sqrt(2.0))))


def _rope128(y, cs, sg):
    lane = lax.broadcasted_iota(jnp.int32, y.shape, 1)
    swapped = jnp.where((lane & 63) < 32, pltpu.roll(y, 96, 1), pltpu.roll(y, 32, 1))
    return y * cs + swapped * sg


def _rmsnorm_kernel(x_ref, g_ref, o_ref):
    o_ref[...] = (_rms(x_ref[...]) * g_ref[...]).astype(o_ref.dtype)


def rmsnorm_cast(x, g):
    m, d = x.shape
    tm = _pick(m, (512, 256, 128))
    return pl.pallas_call(
        _rmsnorm_kernel,
        grid=(m // tm,),
        in_specs=[pl.BlockSpec((tm, d), lambda i: (i, 0)), pl.BlockSpec((1, d), lambda i: (0, 0))],
        out_specs=pl.BlockSpec((tm, d), lambda i: (i, 0)),
        out_shape=jax.ShapeDtypeStruct((m, d), BF16),
        compiler_params=_params("parallel"),
        name="rmsnorm_cast",
    )(x, g.reshape(1, d))


def _mm_kernel(a_ref, w_ref, o_ref):
    o_ref[...] = jnp.dot(a_ref[...].astype(BF16), w_ref[...], preferred_element_type=F32).astype(o_ref.dtype)


def _mm_res_kernel(a_ref, w_ref, x_ref, o_ref):
    o_ref[...] = x_ref[...] + jnp.dot(a_ref[...].astype(BF16), w_ref[...], preferred_element_type=F32)


def matmul(a, w, residual=None, tm_cands=(1024, 512, 256, 128), tn=512, name="matmul"):
    m, k = a.shape
    n = w.shape[1]
    tm = _pick(m, tm_cands)
    in_specs = [pl.BlockSpec((tm, k), lambda i, j: (i, 0)), pl.BlockSpec((k, tn), lambda i, j: (0, j))]
    args = [a, w]
    kern = _mm_kernel
    if residual is not None:
        in_specs.append(pl.BlockSpec((tm, tn), lambda i, j: (i, j)))
        args.append(residual)
        kern = _mm_res_kernel
    return pl.pallas_call(
        kern,
        grid=(m // tm, n // tn),
        in_specs=in_specs,
        out_specs=pl.BlockSpec((tm, tn), lambda i, j: (i, j)),
        out_shape=jax.ShapeDtypeStruct((m, n), F32),
        compiler_params=_params("parallel", "arbitrary"),
        name=name,
    )(*args)


def _swiglu_kernel(a_ref, wg_ref, wu_ref, o_ref):
    a = a_ref[...]
    gate = jnp.dot(a, wg_ref[...], preferred_element_type=F32)
    up = jnp.dot(a, wu_ref[...], preferred_element_type=F32)
    o_ref[...] = (_silu(gate) * up).astype(o_ref.dtype)


def swiglu_up(h, w_gu):
    m, k = h.shape
    hid = w_gu.shape[1] // 2
    tm = _pick(m, (1024, 512, 256, 128))
    tn = 512
    nb = hid // tn
    return pl.pallas_call(
        _swiglu_kernel,
        grid=(m // tm, nb),
        in_specs=[
            pl.BlockSpec((tm, k), lambda i, j: (i, 0)),
            pl.BlockSpec((k, tn), lambda i, j: (0, j)),
            pl.BlockSpec((k, tn), lambda i, j: (0, j + nb)),
        ],
        out_specs=pl.BlockSpec((tm, tn), lambda i, j: (i, j)),
        out_shape=jax.ShapeDtypeStruct((m, hid), BF16),
        compiler_params=_params("parallel", "arbitrary"),
        name="swiglu_up",
    )(h, w_gu, w_gu)


def _merge_kernel(*refs, prompt_blocks):
    a_prompt, a_sample, w_refs, g_refs, o_ref = refs[0:3], refs[3:6], refs[6:9], refs[9:12], refs[12]
    i = pl.program_id(0)

    def run(a_refs):
        acc = None
        for a_ref, w_ref, g_ref in zip(a_refs, w_refs, g_refs):
            y = jnp.dot(a_ref[...].astype(BF16), w_ref[...], preferred_element_type=F32)
            y = jax.nn.sigmoid(g_ref[...]) * y
            acc = y if acc is None else acc + y
        o_ref[...] = acc.astype(o_ref.dtype)

    @pl.when(i < prompt_blocks)
    def _():
        run(a_prompt)

    @pl.when(i >= prompt_blocks)
    def _():
        run(a_sample)


def gated_merge(branches_prompt, branches_sample, weights, z):
    mp, k = branches_prompt[0].shape
    ms = branches_sample[0].shape[0]
    n = weights[0].shape[1]
    tm = _pick(math.gcd(mp, ms), (512, 256, 128))
    tn = 512
    nb = n // tn
    npb = mp // tm
    p_spec = pl.BlockSpec((tm, k), lambda i, j: (jnp.minimum(i, npb - 1), 0))
    s_spec = pl.BlockSpec((tm, k), lambda i, j: (jnp.maximum(i - npb, 0), 0))
    w_spec = pl.BlockSpec((k, tn), lambda i, j: (0, j))
    g_specs = [pl.BlockSpec((tm, tn), functools.partial(lambda i, j, b: (i, Z_GATE // tn + b * nb + j), b=b)) for b in range(N_BRANCH)]
    return pl.pallas_call(
        functools.partial(_merge_kernel, prompt_blocks=npb),
        grid=((mp + ms) // tm, nb),
        in_specs=[p_spec] * 3 + [s_spec] * 3 + [w_spec] * 3 + g_specs,
        out_specs=pl.BlockSpec((tm, tn), lambda i, j: (i, j)),
        out_shape=jax.ShapeDtypeStruct((mp + ms, n), BF16),
        compiler_params=_params("parallel", "arbitrary"),
        name="gated_merge",
    )(*branches_prompt, *branches_sample, *weights, z, z, z)


def _mla_prep_kernel(ql_ref, kv_ref, seg_ref, cs_ref, sg_ref, gq_ref, gkv_ref, gkr_ref, qo_ref, ckv_ref, kr_ref):
    qo_ref[...] = (_rms(ql_ref[...]) * gq_ref[...]).astype(qo_ref.dtype)
    ckv_ref[...] = _rms(kv_ref[...]) * gkv_ref[...]
    seg = seg_ref[...]
    lane = lax.broadcasted_iota(jnp.int32, seg.shape, 1)
    x = jnp.where(lane < MLA_ROPE, seg, 0.0)
    kr_ref[...] = _rope128(_rms(x, MLA_ROPE) * gkr_ref[...], cs_ref[...], sg_ref[...])


def mla_prep(z, cs, sg, g_q, g_kv, g_kr_pad):
    m = z.shape[0]
    tm = _pick(m, (512, 256, 128))
    row = lambda w, c: pl.BlockSpec((tm, w), lambda i: (i, c))
    par = lambda w: pl.BlockSpec((1, w), lambda i: (0, 0))
    return pl.pallas_call(
        _mla_prep_kernel,
        grid=(m // tm,),
        in_specs=[
            row(MLA_Q_RANK, Z_QLAT // MLA_Q_RANK), row(MLA_KV_RANK, Z_KV // MLA_KV_RANK), row(LANES, Z_SEG // LANES),
            row(LANES, 0), row(LANES, 0), par(MLA_Q_RANK), par(MLA_KV_RANK), par(LANES),
        ],
        out_specs=[row(MLA_Q_RANK, 0), row(MLA_KV_RANK, 0), row(LANES, 0)],
        out_shape=[
            jax.ShapeDtypeStruct((m, MLA_Q_RANK), BF16),
            jax.ShapeDtypeStruct((m, MLA_KV_RANK), F32),
            jax.ShapeDtypeStruct((m, LANES), F32),
        ],
        compiler_params=_params("parallel"),
        name="mla_prep",
    )(z, z, z, cs, sg, g_q.reshape(1, -1), g_kv.reshape(1, -1), g_kr_pad)


def _q_kernel(a_ref, w_ref, cs_ref, sg_ref, gn_ref, gr_ref, o_ref, *, scale):
    a = a_ref[...]
    cs, sg = cs_ref[...], sg_ref[...]
    hw = MLA_NOPE + LANES
    q = jnp.dot(a, w_ref[...], preferred_element_type=F32)
    for h in range(MLA_HEADS):
        qn = q[:, h * hw:h * hw + MLA_NOPE]
        o_ref[:, h * hw:h * hw + MLA_NOPE] = (_rms(qn) * gn_ref[...] * scale).astype(o_ref.dtype)
        qr = q[:, h * hw + MLA_NOPE:(h + 1) * hw]
        qr = _rope128(_rms(qr, MLA_ROPE) * gr_ref[...], cs, sg)
        o_ref[:, h * hw + MLA_NOPE:(h + 1) * hw] = (qr * scale).astype(o_ref.dtype)


def q_project(q_lat, w_uq_pad, cs, sg, g_qn, g_qr_pad):
    m, k = q_lat.shape
    n = w_uq_pad.shape[1]
    tm = _pick(m, (512, 256, 128))
    scale = (MLA_NOPE + MLA_ROPE) ** -0.5
    return pl.pallas_call(
        functools.partial(_q_kernel, scale=scale),
        grid=(m // tm,),
        in_specs=[
            pl.BlockSpec((tm, k), lambda i: (i, 0)), pl.BlockSpec((k, n), lambda i: (0, 0)),
            pl.BlockSpec((tm, LANES), lambda i: (i, 0)), pl.BlockSpec((tm, LANES), lambda i: (i, 0)),
            pl.BlockSpec((1, LANES), lambda i: (0, 0)), pl.BlockSpec((1, LANES), lambda i: (0, 0)),
        ],
        out_specs=pl.BlockSpec((tm, n), lambda i: (i, 0)),
        out_shape=jax.ShapeDtypeStruct((m, n), BF16),
        compiler_params=_params("parallel"),
        name="q_project",
    )(q_lat, w_uq_pad, cs, sg, g_qn.reshape(1, -1), g_qr_pad)


def _kv_kernel(a_ref, wk_ref, wv_ref, kr_ref, gk_ref, ko_ref, vo_ref):
    a = a_ref[...].astype(BF16)
    krb = kr_ref[...].astype(BF16)
    hw = MLA_NOPE + LANES
    kn_all = jnp.dot(a, wk_ref[...], preferred_element_type=F32)
    for h in range(MLA_HEADS):
        kn = kn_all[:, h * MLA_NOPE:(h + 1) * MLA_NOPE]
        ko_ref[:, h * hw:h * hw + MLA_NOPE] = (_rms(kn) * gk_ref[...]).astype(ko_ref.dtype)
        ko_ref[:, h * hw + MLA_NOPE:(h + 1) * hw] = krb
    vo_ref[...] = jnp.dot(a, wv_ref[...], preferred_element_type=F32).astype(vo_ref.dtype)


def kv_expand(ckv, kr_pad, w_uk, w_uv, g_kn, rows):
    tm = _pick(rows, (512, 256, 128))
    hw = MLA_NOPE + LANES
    return pl.pallas_call(
        _kv_kernel,
        grid=(rows // tm,),
        in_specs=[
            pl.BlockSpec((tm, MLA_KV_RANK), lambda i: (i, 0)),
            pl.BlockSpec(w_uk.shape, lambda i: (0, 0)), pl.BlockSpec(w_uv.shape, lambda i: (0, 0)),
            pl.BlockSpec((tm, LANES), lambda i: (i, 0)), pl.BlockSpec((1, MLA_NOPE), lambda i: (0, 0)),
        ],
        out_specs=[pl.BlockSpec((tm, MLA_HEADS * hw), lambda i: (i, 0)), pl.BlockSpec((tm, MLA_HEADS * MLA_V), lambda i: (i, 0))],
        out_shape=[jax.ShapeDtypeStruct((rows, MLA_HEADS * hw), BF16), jax.ShapeDtypeStruct((rows, MLA_HEADS * MLA_V), BF16)],
        compiler_params=_params("parallel"),
        name="kv_expand",
    )(ckv, w_uk, w_uv, kr_pad, g_kn.reshape(1, -1))


def _flash_kernel(q_ref, k_ref, v_ref, o_ref, *, tq, nq):
    qi = pl.program_id(2)
    for i in range(nq):
        @pl.when(qi == i)
        def _(i=i):
            kend = (i + 1) * tq
            s = lax.dot_general(q_ref[...], k_ref[0:kend, :], NT, preferred_element_type=F32)
            row = i * tq + lax.broadcasted_iota(jnp.int32, s.shape, 0)
            col = lax.broadcasted_iota(jnp.int32, s.shape, 1)
            s = jnp.where(col <= row, s, -jnp.inf)
            p = jnp.exp(s - jnp.max(s, axis=-1, keepdims=True))
            l = jnp.sum(p, axis=-1, keepdims=True)
            o = jnp.dot(p.astype(BF16), v_ref[0:kend, :], preferred_element_type=F32)
            o_ref[...] = (o / l).astype(o_ref.dtype)


def mla_prompt_attention(qcat, kcat, v, batch, seq):
    tq = _pick(seq, (256, 128))
    nq = seq // tq
    hw = MLA_NOPE + LANES
    return pl.pallas_call(
        functools.partial(_flash_kernel, tq=tq, nq=nq),
        grid=(batch, MLA_HEADS, nq),
        in_specs=[
            pl.BlockSpec((tq, hw), lambda b, h, i: (b * nq + i, h)),
            pl.BlockSpec((seq, hw), lambda b, h, i: (b, h)),
            pl.BlockSpec((seq, MLA_V), lambda b, h, i: (b, h)),
        ],
        out_specs=pl.BlockSpec((tq, MLA_V), lambda b, h, i: (b * nq + i, h)),
        out_shape=jax.ShapeDtypeStruct((batch * seq, MLA_HEADS * MLA_V), BF16),
        compiler_params=_params("parallel", "parallel", "arbitrary"),
        name="mla_prompt_attention",
    )(qcat, kcat, v)


def _absorb_kernel(q_ref, w_ref, g_ref, o_ref):
    q = q_ref[:, 0:MLA_NOPE].astype(F32) * g_ref[...]
    o_ref[0] = jnp.dot(q.astype(BF16), w_ref[0], preferred_element_type=F32)


def absorb_queries(qcat, w_ukt, g_kn, row0, rows):
    hw = MLA_NOPE + LANES
    return pl.pallas_call(
        _absorb_kernel,
        grid=(MLA_HEADS,),
        in_specs=[
            pl.BlockSpec((rows, hw), lambda h: (row0 // rows, h)),
            pl.BlockSpec((1, MLA_NOPE, MLA_KV_RANK), lambda h: (h, 0, 0)),
            pl.BlockSpec((1, MLA_NOPE), lambda h: (0, 0)),
        ],
        out_specs=pl.BlockSpec((1, rows, MLA_KV_RANK), lambda h: (h, 0, 0)),
        out_shape=jax.ShapeDtypeStruct((MLA_HEADS, rows, MLA_KV_RANK), F32),
        compiler_params=_params("parallel"),
        name="absorb_queries",
    )(qcat, w_ukt, g_kn.reshape(1, -1))


def _unabsorb_kernel(a_ref, w_ref, o_ref):
    o_ref[...] = jnp.dot(a_ref[0].astype(BF16), w_ref[0], preferred_element_type=F32)


def unabsorb_values(o_lat, w_uv_h):
    _, rows, r = o_lat.shape
    return pl.pallas_call(
        _unabsorb_kernel,
        grid=(MLA_HEADS,),
        in_specs=[pl.BlockSpec((1, rows, r), lambda h: (h, 0, 0)), pl.BlockSpec((1, r, MLA_V), lambda h: (h, 0, 0))],
        out_specs=pl.BlockSpec((rows, MLA_V), lambda h: (0, h)),
        out_shape=jax.ShapeDtypeStruct((rows, MLA_HEADS * MLA_V), F32),
        compiler_params=_params("parallel"),
        name="unabsorb_values",
    )(o_lat, w_uv_h)


def _sattn_kernel(pt_ref, qa_ref, qr_ref, cn_ref, kn_ref, wk_ref, ckv_hbm, krt_hbm, o_ref,
                  ckv_buf, krt_buf, ckb, s_sc, sem, *, layer, tile_pages, n_pages, page, dec):
    b = pl.program_id(0)
    nb = pl.num_programs(0)
    slot = b % 2
    ntiles = n_pages // tile_pages
    tk = tile_pages * page

    def tile_copies(bb, j, sl, from_table=True):
        out = []
        for g in range(tile_pages):
            pg = pt_ref[bb * n_pages + j * tile_pages + g] if from_table else 0
            row = pl.multiple_of((j * tile_pages + g) * page, page)
            out.append(pltpu.make_async_copy(ckv_hbm.at[layer, pg], ckv_buf.at[sl, pl.ds(row, page)], sem.at[sl, 0]))
            out.append(pltpu.make_async_copy(krt_hbm.at[layer, pg], krt_buf.at[sl, j, :, pl.ds(g * page, page)], sem.at[sl, 1]))
        return out

    @pl.when(b == 0)
    def _():
        def first(j, carry):
            for cp in tile_copies(b, j, slot):
                cp.start()
            return carry
        lax.fori_loop(0, ntiles, first, 0)

    def drain(j, carry):
        for cp in tile_copies(b, j, slot, from_table=False):
            cp.wait()
        return carry
    lax.fori_loop(0, ntiles, drain, 0)

    qa = qa_ref[0].astype(BF16)
    qr = qr_ref[0].astype(BF16)
    wk = wk_ref[...]

    def nope_scores(ck):
        kn = lax.dot_general(wk, ck, NT, preferred_element_type=F32)
        raw = lax.dot_general(qa, ck, NT, preferred_element_type=F32)
        inv = []
        for h in range(MLA_HEADS):
            kh = kn[h * MLA_NOPE:(h + 1) * MLA_NOPE]
            r = lax.rsqrt(jnp.sum(kh * kh, axis=0, keepdims=True) * (1.0 / MLA_NOPE) + EPS)
            inv.append(jnp.broadcast_to(r, (dec, r.shape[1])))
        return raw * jnp.concatenate(inv, axis=0)

    def tile(j, carry):
        @pl.when(b + 1 < nb)
        def _():
            for cp in tile_copies(b + 1, j, 1 - slot):
                cp.start()

        r0 = pl.multiple_of(j * tk, tk)
        ck = ckv_buf[slot, pl.ds(r0, tk), :].astype(BF16)
        ckb[pl.ds(r0, tk), :] = ck
        rope = jnp.dot(qr, krt_buf[slot, j].astype(BF16), preferred_element_type=F32)
        s_sc[j] = nope_scores(ck) + rope
        return carry
    lax.fori_loop(0, ntiles, tile, 0)

    pad = page - dec
    cnew = jnp.concatenate([cn_ref[...], jnp.zeros((pad, cn_ref.shape[1]), F32)], axis=0).astype(BF16)
    knew = jnp.concatenate([kn_ref[:, 0:MLA_ROPE], jnp.zeros((pad, MLA_ROPE), F32)], axis=0).astype(BF16)
    s_new = nope_scores(cnew) + lax.dot_general(qr, knew, NT, preferred_element_type=F32)
    qpos = lax.broadcasted_iota(jnp.int32, s_new.shape, 0) % dec
    kpos = lax.broadcasted_iota(jnp.int32, s_new.shape, 1)
    s_new = jnp.where(kpos <= qpos, s_new, -jnp.inf)

    m = jnp.max(s_new, axis=-1, keepdims=True)
    for j in range(ntiles):
        m = jnp.maximum(m, jnp.max(s_sc[j], axis=-1, keepdims=True))
    p = jnp.exp(s_new - m)
    l = jnp.sum(p, axis=-1, keepdims=True)
    acc = jnp.dot(p.astype(BF16), cnew, preferred_element_type=F32)
    for j in range(ntiles):
        p = jnp.exp(s_sc[j] - m)
        l = l + jnp.sum(p, axis=-1, keepdims=True)
        acc = acc + jnp.dot(p.astype(BF16), ckb[j * tk:(j + 1) * tk, :], preferred_element_type=F32)
    o_ref[0] = acc / l


def mla_sample_attention(q_abs, q_rope, ckv, kr_pad, w_ukt2, cache_ckv, cache_krope_t, page_table, layer, row0, dec):
    nb, nq, r = q_abs.shape
    n_pages = page_table.shape[1]
    page = cache_ckv.shape[2]
    tile_pages = _pick(n_pages, (8, 4, 2, 1))
    ntiles = n_pages // tile_pages
    tk = tile_pages * page
    grid_spec = pltpu.PrefetchScalarGridSpec(
        num_scalar_prefetch=1,
        grid=(nb,),
        in_specs=[
            pl.BlockSpec((1, nq, r), lambda b, pt: (b, 0, 0)),
            pl.BlockSpec((1, nq, MLA_ROPE), lambda b, pt: (b, 0, 0)),
            pl.BlockSpec((dec, r), lambda b, pt: (row0 // dec + b, 0)),
            pl.BlockSpec((dec, LANES), lambda b, pt: (row0 // dec + b, 0)),
            pl.BlockSpec(w_ukt2.shape, lambda b, pt: (0, 0)),
            pl.BlockSpec(memory_space=pl.ANY),
            pl.BlockSpec(memory_space=pl.ANY),
        ],
        out_specs=pl.BlockSpec((1, nq, r), lambda b, pt: (b, 0, 0)),
        scratch_shapes=[
            pltpu.VMEM((2, n_pages * page, r), F32),
            pltpu.VMEM((2, ntiles, MLA_ROPE, tk), F32),
            pltpu.VMEM((n_pages * page, r), BF16),
            pltpu.VMEM((ntiles, nq, tk), F32),
            pltpu.SemaphoreType.DMA((2, 2)),
        ],
    )
    return pl.pallas_call(
        functools.partial(_sattn_kernel, layer=layer, tile_pages=tile_pages, n_pages=n_pages, page=page, dec=dec),
        grid_spec=grid_spec,
        out_shape=jax.ShapeDtypeStruct((nb, nq, r), F32),
        compiler_params=_params("arbitrary"),
        name="mla_sample_attention",
    )(page_table.reshape(-1), q_abs, q_rope, ckv, kr_pad, w_ukt2, cache_ckv, cache_krope_t)


def _chunk_masks(c):
    ii = lax.broadcasted_iota(jnp.int32, (c, c), 0)
    jj = lax.broadcasted_iota(jnp.int32, (c, c), 1)
    return jj <= ii, jj < ii


def _split_bf16(x, terms):
    parts = []
    for _ in range(terms):
        p = x.astype(BF16)
        parts.append(p)
        x = x - p.astype(F32)
    return parts


def _dot_split(a, b):
    ah, al = _split_bf16(a, 2)
    bh, bl = _split_bf16(b, 2)
    dot = functools.partial(jnp.dot, preferred_element_type=F32)
    return dot(ah, bh) + (dot(ah, bl) + dot(al, bh))


def _gdn_prep_kernel(x_ref, seg_ref, c0_ref, cw_ref, al_ref, dt_ref, qkv_ref, gb_ref, cn_ref, a_ref, ext, *, nsq, rows, chunk):
    t = pl.program_id(1)
    hk = GDN_HEADS * GDN_DK
    tail = GDN_CONV - 1
    tb = nsq * rows
    cw = cw_ref[...]
    for sq in range(nsq):
        @pl.when(t == 0)
        def _(sq=sq):
            ext[sq, 8 - tail:8, :] = c0_ref[sq]

        ext[sq, 8:8 + rows, :] = x_ref[sq * rows:(sq + 1) * rows, :]
        y = cw[0:1] * ext[sq, 5:5 + rows, :]
        for j in range(1, GDN_CONV):
            y = y + cw[j:j + 1] * ext[sq, 5 + j:5 + j + rows, :]
        new_tail = ext[sq, 8 + rows - tail:8 + rows, :]
        cn_ref[sq] = new_tail
        ext[sq, 8 - tail:8, :] = new_tail
        qkv_ref[sq * rows:(sq + 1) * rows, :] = _silu(y)

    seg = seg_ref[...]
    lane = lax.broadcasted_iota(jnp.int32, seg.shape, 1)
    beta = jax.nn.sigmoid(seg)
    g = -jnp.exp(al_ref[...]) * jax.nn.softplus(seg + dt_ref[...])
    ri = lax.broadcasted_iota(jnp.int32, (tb, tb), 0)
    ci = lax.broadcasted_iota(jnp.int32, (tb, tb), 1)
    cum = jnp.where((ri // chunk == ci // chunk) & (ci <= ri), 1.0, 0.0).astype(BF16)
    gcum = sum(jnp.dot(cum, p, preferred_element_type=F32) for p in reversed(_split_bf16(g, 3)))
    gb = jnp.where((lane >= SEG_A) & (lane < SEG_B), gcum, beta)
    gb_ref[...] = gb
    gcum_t = gcum.T

    _, strict = _chunk_masks(chunk)
    for h in range(GDN_HEADS):
        qh = qkv_ref[:, h * GDN_DK:(h + 1) * GDN_DK]
        qh = qh * lax.rsqrt(jnp.sum(qh * qh, axis=-1, keepdims=True) + EPS) * (GDN_DK ** -0.5)
        kh = qkv_ref[:, hk + h * GDN_DK:hk + (h + 1) * GDN_DK]
        kh = kh * lax.rsqrt(jnp.sum(kh * kh, axis=-1, keepdims=True) + EPS)
        qkv_ref[:, h * GDN_DK:(h + 1) * GDN_DK] = qh
        qkv_ref[:, hk + h * GDN_DK:hk + (h + 1) * GDN_DK] = kh
        kb = (kh * beta[:, SEG_B + h:SEG_B + h + 1]).astype(BF16)
        kh = kh.astype(BF16)
        for c in range(tb // chunk):
            r0, r1 = c * chunk, (c + 1) * chunk
            gc = gcum[r0:r1, SEG_A + h:SEG_A + h + 1]
            gr = gcum_t[SEG_A + h:SEG_A + h + 1, r0:r1]
            decay = jnp.exp(jnp.where(strict, gc - gr, 0.0))
            kk = lax.dot_general(kb[r0:r1], kh[r0:r1], NT, preferred_element_type=F32)
            a_ref[c, h] = jnp.where(strict, kk * decay, 0.0)


def gdn_prep(z, conv0, conv_w, a_log_row, dt_row, row0, nseq, seqlen, nsq, rows, chunk):
    nt = seqlen // rows
    tb = nsq * rows
    nc = tb // chunk
    base = row0 // tb
    assert nsq == 1 or nt == 1
    return pl.pallas_call(
        functools.partial(_gdn_prep_kernel, nsq=nsq, rows=rows, chunk=chunk),
        grid=(nseq // nsq, nt),
        in_specs=[
            pl.BlockSpec((tb, GDN_QKV), lambda b, t: (base + b * nt + t, Z_QKV // GDN_QKV)),
            pl.BlockSpec((tb, LANES), lambda b, t: (base + b * nt + t, Z_SEG // LANES)),
            pl.BlockSpec((nsq, GDN_CONV - 1, GDN_QKV), lambda b, t: (b, 0, 0)),
            pl.BlockSpec((GDN_CONV, GDN_QKV), lambda b, t: (0, 0)),
            pl.BlockSpec((1, LANES), lambda b, t: (0, 0)),
            pl.BlockSpec((1, LANES), lambda b, t: (0, 0)),
        ],
        out_specs=[
            pl.BlockSpec((tb, GDN_QKV), lambda b, t: (b * nt + t, 0)),
            pl.BlockSpec((tb, LANES), lambda b, t: (b * nt + t, 0)),
            pl.BlockSpec((nsq, GDN_CONV - 1, GDN_QKV), lambda b, t: (b, 0, 0)),
            pl.BlockSpec((nc, GDN_HEADS, chunk, chunk), lambda b, t: (b * nt + t, 0, 0, 0)),
        ],
        out_shape=[
            jax.ShapeDtypeStruct((nseq * seqlen, GDN_QKV), F32),
            jax.ShapeDtypeStruct((nseq * seqlen, LANES), F32),
            jax.ShapeDtypeStruct((nseq, GDN_CONV - 1, GDN_QKV), F32),
            jax.ShapeDtypeStruct((nseq * seqlen // chunk, GDN_HEADS, chunk, chunk), F32),
        ],
        scratch_shapes=[pltpu.VMEM((nsq, rows + 8, GDN_QKV), F32)],
        compiler_params=_params("parallel", "arbitrary"),
        name="gdn_prep",
    )(z, z, conv0, conv_w, a_log_row, dt_row)


def _trinv_kernel(a_ref, o_ref, t_sc, *, n, kb):
    i = pl.program_id(0)
    for b in range(n // kb):
        k0 = b * kb

        def body(j, acc, k0=k0):
            return acc + a_ref[0, j][None] * t_sc[j, k0:k0 + kb]

        acc = lax.fori_loop(k0, i, body, jnp.zeros((kb,) + t_sc.shape[2:], F32))
        kidx = k0 + lax.broadcasted_iota(jnp.int32, acc.shape, 0)
        row = jnp.where(kidx == i, 1.0, 0.0) - acc
        t_sc[i, k0:k0 + kb] = row
        o_ref[0, k0:k0 + kb] = row


def unit_lower_inverse(a):
    nsys, n, _ = a.shape
    tile = 8 * LANES
    npad = -(-nsys // tile) * tile
    at = jnp.transpose(a, (1, 2, 0))
    if npad != nsys:
        at = jnp.pad(at, ((0, 0), (0, 0), (0, npad - nsys)))
    at = at.reshape(n, n, npad // LANES, LANES)
    kb = min(n, 16)
    outs = []
    for s in range(npad // tile):
        blk = at[:, :, s * 8:(s + 1) * 8]
        outs.append(pl.pallas_call(
            functools.partial(_trinv_kernel, n=n, kb=kb),
            grid=(n,),
            in_specs=[pl.BlockSpec((1, n, 8, LANES), lambda i: (i, 0, 0, 0))],
            out_specs=pl.BlockSpec((1, n, 8, LANES), lambda i: (i, 0, 0, 0)),
            out_shape=jax.ShapeDtypeStruct((n, n, 8, LANES), F32),
            scratch_shapes=[pltpu.VMEM((n, n, 8, LANES), F32)],
            compiler_params=_params("arbitrary"),
            name="unit_lower_inverse",
        )(blk))
    t = outs[0] if len(outs) == 1 else jnp.concatenate(outs, axis=2)
    t = t.reshape(n, n, npad)[:, :, :nsys]
    return jnp.transpose(t, (2, 0, 1))


def _gdn_scan_kernel(qkv_ref, gb_ref, z_ref, ti_ref, s0_ref, gon_ref, o_ref, so_ref, s_sc, *, nsq, rows, chunk):
    t = pl.program_id(1)
    hk = GDN_HEADS * GDN_DK

    @pl.when(t == 0)
    def _():
        s_sc[...] = s0_ref[...]

    gb = gb_ref[...]
    gb_t = gb.T
    incl, _ = _chunk_masks(chunk)
    ncs = rows // chunk
    pairs = [(sq, h) for sq in range(nsq) for h in range(GDN_HEADS)]
    dot = functools.partial(jnp.dot, preferred_element_type=F32)
    for c in range(ncs):
        span = {(sq, h): (sq * rows + c * chunk, sq * rows + (c + 1) * chunk) for sq, h in pairs}
        st1, st2, st3, st4 = {}, {}, {}, {}
        for sq, h in pairs:
            r0, r1 = span[sq, h]
            gc = gb[r0:r1, SEG_A + h:SEG_A + h + 1]
            gr = gb_t[SEG_A + h:SEG_A + h + 1, r0:r1]
            beta = gb[r0:r1, SEG_B + h:SEG_B + h + 1]
            q = qkv_ref[r0:r1, h * GDN_DK:(h + 1) * GDN_DK]
            k = qkv_ref[r0:r1, hk + h * GDN_DK:hk + (h + 1) * GDN_DK]
            v = qkv_ref[r0:r1, 2 * hk + h * GDN_DV:2 * hk + (h + 1) * GDN_DV]
            egc = jnp.exp(gc)
            glast = gc[chunk - 1:chunk, :]
            th, tl = _split_bf16(ti_ref[sq * ncs + c, h], 2)
            rhs = jnp.concatenate([v * beta, k * (beta * egc)], axis=1)
            rh, rl = _split_bf16(rhs, 2)
            sol = (dot(th, rh), dot(th, rl), dot(tl, rh))
            qk = lax.dot_general(q.astype(BF16), k.astype(BF16), NT, preferred_element_type=F32)
            decay = jnp.exp(jnp.where(incl, gc - gr, 0.0))
            k_tail = (k * jnp.exp(glast - gc)).astype(BF16)
            st1[sq, h] = (sol, qk, decay, q * egc, k_tail, jnp.exp(glast))
        for sq, h in pairs:
            sol, qk, decay, qe, k_tail, dec_last = st1[sq, h]
            uw = sol[0] + (sol[1] + sol[2])
            wq = jnp.concatenate([uw[:, GDN_DV:], qe], axis=0).astype(BF16)
            attn = jnp.where(incl, qk * decay, 0.0).astype(BF16)
            st2[sq, h] = (uw[:, :GDN_DV], wq, attn, k_tail, dec_last)
        for sq, h in pairs:
            u, wq, attn, k_tail, dec_last = st2[sq, h]
            s_h = s_sc[sq, h]
            st3[sq, h] = (u, dot(wq, s_h.astype(BF16)), attn, k_tail, s_h * dec_last)
        for sq, h in pairs:
            u, ws, attn, k_tail, s_dec = st3[sq, h]
            v_new_b = (u - ws[:chunk]).astype(BF16)
            st4[sq, h] = ws[chunk:] + dot(attn, v_new_b)
            s_sc[sq, h] = s_dec + lax.dot_general(k_tail, v_new_b, TN, preferred_element_type=F32)
        for sq, h in pairs:
            r0, r1 = span[sq, h]
            zg = z_ref[r0:r1, h * GDN_DV:(h + 1) * GDN_DV]
            o_ref[r0:r1, h * GDN_DV:(h + 1) * GDN_DV] = _rms(st4[sq, h]) * gon_ref[...] * _silu(zg)

    @pl.when(t == pl.num_programs(1) - 1)
    def _():
        so_ref[...] = s_sc[...]


def gdn_scan(qkv, gb, z, tinv, s0, g_onorm, row0, nseq, seqlen, nsq, rows, chunk):
    nt = seqlen // rows
    tb = nsq * rows
    nc = tb // chunk
    base = row0 // tb
    state = (nsq, GDN_HEADS, GDN_DK, GDN_DV)
    return pl.pallas_call(
        functools.partial(_gdn_scan_kernel, nsq=nsq, rows=rows, chunk=chunk),
        grid=(nseq // nsq, nt),
        in_specs=[
            pl.BlockSpec((tb, GDN_QKV), lambda b, t: (b * nt + t, 0)),
            pl.BlockSpec((tb, LANES), lambda b, t: (b * nt + t, 0)),
            pl.BlockSpec((tb, GDN_W), lambda b, t: (base + b * nt + t, Z_Z // GDN_W)),
            pl.BlockSpec((nc, GDN_HEADS, chunk, chunk), lambda b, t: (b * nt + t, 0, 0, 0)),
            pl.BlockSpec(state, lambda b, t: (b, 0, 0, 0)),
            pl.BlockSpec((1, GDN_DV), lambda b, t: (0, 0)),
        ],
        out_specs=[
            pl.BlockSpec((tb, GDN_W), lambda b, t: (b * nt + t, 0)),
            pl.BlockSpec(state, lambda b, t: (b, 0, 0, 0)),
        ],
        out_shape=[
            jax.ShapeDtypeStruct((nseq * seqlen, GDN_W), F32),
            jax.ShapeDtypeStruct((nseq, GDN_HEADS, GDN_DK, GDN_DV), F32),
        ],
        scratch_shapes=[pltpu.VMEM(state, F32)],
        compiler_params=_params("parallel", "arbitrary"),
        name="gdn_scan",
    )(qkv, gb, z, tinv, s0, g_onorm.reshape(1, -1))


def gdn_branch(z, conv0, s0, conv_w, a_log_row, dt_row, g_onorm, row0, nseq, seqlen):
    chunk = min(GDN_CHUNK, seqlen)
    rows = _pick(seqlen, (256, 128, 64, 8))
    nsq = _pick(nseq, (8, 4, 2, 1)) if rows == seqlen and rows < GDN_CHUNK else 1
    qkv, gb, conv_new, a = gdn_prep(z, conv0, conv_w, a_log_row, dt_row, row0, nseq, seqlen, nsq, rows, chunk)
    tinv = unit_lower_inverse(a.reshape(-1, chunk, chunk)).reshape(a.shape)
    o, s_new = gdn_scan(qkv, gb, z, tinv, s0, g_onorm, row0, nseq, seqlen, nsq, rows, chunk)
    return o, conv_new, s_new


def _layer_norm(v, g, b):
    mu = jnp.mean(v, axis=-1, keepdims=True)
    d = v - mu
    return d * lax.rsqrt(jnp.mean(d * d, axis=-1, keepdims=True) + EPS) * g + b


def _gmlp_prompt_kernel(u_ref, v_ref, lg_ref, lb_ref, w_ref, bias_ref, o_ref, *, rows):
    vn = _layer_norm(_gelu(v_ref[...]), lg_ref[...], lb_ref[...]).astype(BF16)
    u = _gelu(u_ref[...])
    tril, _ = _chunk_masks(GMLP_CHUNK)
    bias = bias_ref[...]
    blocks = [(g, c) for g in range(GMLP_GROUPS) for c in range(rows // GMLP_CHUNK)]
    wg = [jnp.where(tril, w_ref[g], 0.0).astype(BF16) for g in range(GMLP_GROUPS)]
    s = {}
    for g, c in blocks:
        s[g, c] = jnp.dot(wg[g], vn[c * GMLP_CHUNK:(c + 1) * GMLP_CHUNK, g * GMLP_GROUP_W:(g + 1) * GMLP_GROUP_W], preferred_element_type=F32)
    for g, c in blocks:
        r0, r1 = c * GMLP_CHUNK, (c + 1) * GMLP_CHUNK
        c0, c1 = g * GMLP_GROUP_W, (g + 1) * GMLP_GROUP_W
        o_ref[r0:r1, c0:c1] = (u[r0:r1, c0:c1] * (s[g, c] + bias[:, c0:c1])).astype(o_ref.dtype)


def gmlp_prompt(z, ln_g, ln_b, w_s, bias_full, nrows):
    rows = _pick(nrows, (512, 256, 128))
    row = lambda cb: pl.BlockSpec((rows, GMLP_W), lambda i: (i, cb))
    par = lambda: pl.BlockSpec((1, GMLP_W), lambda i: (0, 0))
    return pl.pallas_call(
        functools.partial(_gmlp_prompt_kernel, rows=rows),
        grid=(nrows // rows,),
        in_specs=[
            row(Z_U // GMLP_W), row(Z_V // GMLP_W), par(), par(),
            pl.BlockSpec(w_s.shape, lambda i: (0, 0, 0)),
            pl.BlockSpec(bias_full.shape, lambda i: (0, 0)),
        ],
        out_specs=row(0),
        out_shape=jax.ShapeDtypeStruct((nrows, GMLP_W), BF16),
        compiler_params=_params("parallel"),
        name="gmlp_prompt",
    )(z, z, ln_g.reshape(1, -1), ln_b.reshape(1, -1), w_s, bias_full)


def _gmlp_sample_kernel(u_ref, v_ref, lg_ref, lb_ref, wt_ref, b_ref, o_ref, vr_ref, *, rows, dec):
    vn = _layer_norm(_gelu(v_ref[...]), lg_ref[...], lb_ref[...])
    vr_ref[...] = vn
    v3 = vn.reshape(rows // dec, dec, GMLP_W)
    tpos = lax.broadcasted_iota(jnp.int32, (dec, GMLP_W), 0)
    s3 = jnp.broadcast_to(b_ref[...][None], v3.shape)
    for j in range(dec):
        wj = jnp.where(tpos >= j, wt_ref[j], 0.0)
        s3 = s3 + wj[None] * v3[:, j:j + 1, :]
    u3 = _gelu(u_ref[...]).reshape(rows // dec, dec, GMLP_W)
    o_ref[...] = (u3 * s3).reshape(rows, GMLP_W)


def gmlp_sample(z, ln_g, ln_b, wt, b_dec, row0, nrows, dec):
    rows = _pick(nrows, (256, 128))
    base = row0 // rows
    row = lambda cb, off: pl.BlockSpec((rows, GMLP_W), lambda i: (off + i, cb))
    par = lambda: pl.BlockSpec((1, GMLP_W), lambda i: (0, 0))
    return pl.pallas_call(
        functools.partial(_gmlp_sample_kernel, rows=rows, dec=dec),
        grid=(nrows // rows,),
        in_specs=[
            row(Z_U // GMLP_W, base), row(Z_V // GMLP_W, base), par(), par(),
            pl.BlockSpec(wt.shape, lambda i: (0, 0, 0)),
            pl.BlockSpec(b_dec.shape, lambda i: (0, 0)),
        ],
        out_specs=[row(0, 0), row(0, 0)],
        out_shape=[jax.ShapeDtypeStruct((nrows, GMLP_W), F32), jax.ShapeDtypeStruct((nrows, GMLP_W), F32)],
        compiler_params=_params("parallel"),
        name="gmlp_sample",
    )(z, z, ln_g.reshape(1, -1), ln_b.reshape(1, -1), wt, b_dec)


def _rope_tables(pos):
    inv = ROPE_THETA ** (-jnp.arange(0, MLA_ROPE, 2, dtype=F32) / MLA_ROPE)
    ang = pos.astype(F32)[:, None] * inv[None, :]
    cos, sin = jnp.cos(ang), jnp.sin(ang)
    zero = jnp.zeros((pos.shape[0], LANES - MLA_ROPE), F32)
    return jnp.concatenate([cos, cos, zero], axis=1), jnp.concatenate([-sin, sin, zero], axis=1)


def _pad_lanes(v, offset=0):
    return jnp.zeros((1, LANES), F32).at[0, offset:offset + v.shape[0]].set(v)


def kernel(x_prompt, x_sample, cache_ckv, cache_krope, state_gdn, state_conv, page_table, norm_mix_g, w_in, mla_q_norm_g, mla_w_uq, mla_qn_g, mla_qr_g, mla_kv_norm_g, mla_kr_g, mla_w_uk, mla_kn_g, mla_w_uv, gdn_conv_w, gdn_a_log, gdn_dt_bias, gdn_o_norm_g, gmlp_ln_g, gmlp_ln_b, gmlp_w_s, gmlp_b_s, w_br_mla, w_br_gdn, w_br_gmlp, w_o, norm_ffn_g, ffn_w_gu, ffn_w_down):
    bp, seq, d = x_prompt.shape
    bs, dec, _ = x_sample.shape
    depth = w_in.shape[0]
    mp, ms = bp * seq, bs * dec
    n_past = page_table.shape[1] * cache_ckv.shape[2]
    assert d == D_MODEL and dec == 8 and mp % ms == 0 and seq % GMLP_CHUNK == 0 and seq % GDN_CHUNK == 0

    x = jnp.concatenate([x_prompt.reshape(mp, d), x_sample.reshape(ms, d)], axis=0)
    pos = jnp.concatenate([jnp.tile(jnp.arange(seq), bp), jnp.tile(n_past + jnp.arange(dec), bs)])
    cs, sg = _rope_tables(pos)
    cache_krope_t = jnp.swapaxes(cache_krope, 2, 3)
    conv0_p = jnp.zeros((bp, GDN_CONV - 1, GDN_QKV), F32)
    s0_p = jnp.zeros((bp, GDN_HEADS, GDN_DK, GDN_DV), F32)

    outs = [[] for _ in range(9)]
    for l in range(depth):
        wi = w_in[l]
        o = [0]
        for n in (MLA_Q_RANK, MLA_KV_RANK, MLA_ROPE, GDN_QKV, GDN_W, GDN_HEADS, GDN_HEADS, GMLP_W, GMLP_W, N_BRANCH * D_MODEL):
            o.append(o[-1] + n)
        w_in_p = jnp.concatenate([
            wi[:, o[9]:o[10]], wi[:, o[3]:o[4]], wi[:, o[7]:o[8]], wi[:, o[8]:o[9]], wi[:, o[4]:o[5]],
            wi[:, o[0]:o[1]], wi[:, o[1]:o[2]], wi[:, o[2]:o[3]], wi[:, o[5]:o[7]],
            jnp.zeros((d, Z_COLS - Z_SEG - MLA_ROPE - 2 * GDN_HEADS), F32),
        ], axis=1).astype(BF16)
        w_uq = mla_w_uq[l].reshape(MLA_Q_RANK, MLA_HEADS, MLA_NOPE + MLA_ROPE)
        w_uq = jnp.concatenate([w_uq, jnp.zeros((MLA_Q_RANK, MLA_HEADS, LANES - MLA_ROPE), F32)], axis=-1)
        w_uq = w_uq.reshape(MLA_Q_RANK, -1).astype(BF16)
        w_uk = mla_w_uk[l].reshape(MLA_KV_RANK, -1).astype(BF16)
        w_uv = mla_w_uv[l].reshape(MLA_KV_RANK, -1).astype(BF16)
        w_ukt = jnp.transpose(mla_w_uk[l], (1, 2, 0)).astype(BF16)
        w_uv_h = jnp.transpose(mla_w_uv[l], (1, 0, 2)).astype(BF16)
        g_kr_pad = _pad_lanes(mla_kr_g[l])
        g_qr_pad = _pad_lanes(mla_qr_g[l])
        a_log_row = _pad_lanes(gdn_a_log[l], SEG_A)
        dt_row = _pad_lanes(gdn_dt_bias[l], SEG_A)
        bias_full = jnp.repeat(gmlp_b_s[l].T, GMLP_GROUP_W, axis=1)
        wt_dec = jnp.repeat(jnp.transpose(gmlp_w_s[l][:, :dec, :dec], (2, 1, 0)), GMLP_GROUP_W, axis=2)

        h = rmsnorm_cast(x, norm_mix_g[l])
        z = matmul(h, w_in_p, name="in_proj")

        q_lat, ckv, kr_pad = mla_prep(z, cs, sg, mla_q_norm_g[l], mla_kv_norm_g[l], g_kr_pad)
        qcat = q_project(q_lat, w_uq, cs, sg, mla_qn_g[l], g_qr_pad)
        kcat, vexp = kv_expand(ckv, kr_pad, w_uk, w_uv, mla_kn_g[l], mp)
        o_mla_p = mla_prompt_attention(qcat, kcat, vexp, bp, seq)
        q_abs = absorb_queries(qcat, w_ukt, mla_kn_g[l], mp, ms)
        q_abs = jnp.transpose(q_abs.reshape(MLA_HEADS, bs, dec, MLA_KV_RANK), (1, 0, 2, 3)).reshape(bs, MLA_HEADS * dec, MLA_KV_RANK)
        q_rope = qcat[mp:].reshape(bs, dec, MLA_HEADS, MLA_NOPE + LANES)[..., MLA_NOPE:MLA_NOPE + MLA_ROPE]
        q_rope = jnp.transpose(q_rope, (0, 2, 1, 3)).reshape(bs, MLA_HEADS * dec, MLA_ROPE).astype(F32)
        o_lat = mla_sample_attention(q_abs, q_rope, ckv, kr_pad, w_ukt.reshape(-1, MLA_KV_RANK), cache_ckv, cache_krope_t, page_table, l, mp, dec)
        o_lat = jnp.transpose(o_lat.reshape(bs, MLA_HEADS, dec, MLA_KV_RANK), (1, 0, 2, 3)).reshape(MLA_HEADS, ms, MLA_KV_RANK)
        o_mla_s = unabsorb_values(o_lat, w_uv_h)

        o_gdn_p, conv_p, s_p = gdn_branch(z, conv0_p, s0_p, gdn_conv_w[l], a_log_row, dt_row, gdn_o_norm_g[l], 0, bp, seq)
        o_gdn_s, conv_s, s_s = gdn_branch(z, state_conv[l], state_gdn[l], gdn_conv_w[l], a_log_row, dt_row, gdn_o_norm_g[l], mp, bs, dec)

        o_gmlp_p = gmlp_prompt(z, gmlp_ln_g[l], gmlp_ln_b[l], gmlp_w_s[l], bias_full, mp)
        o_gmlp_s, v_rows = gmlp_sample(z, gmlp_ln_g[l], gmlp_ln_b[l], wt_dec, bias_full[:dec], mp, ms, dec)

        merged = gated_merge(
            (o_mla_p, o_gdn_p, o_gmlp_p), (o_mla_s, o_gdn_s, o_gmlp_s),
            (w_br_mla[l].astype(BF16), w_br_gdn[l].astype(BF16), w_br_gmlp[l].astype(BF16)), z)
        x = matmul(merged, w_o[l].astype(BF16), residual=x, name="out_proj")
        h2 = rmsnorm_cast(x, norm_ffn_g[l])
        act = swiglu_up(h2, ffn_w_gu[l].astype(BF16))
        x = matmul(act, ffn_w_down[l].astype(BF16), residual=x, tm_cands=(512, 256, 128), name="ffn_down")

        kr = kr_pad[:, :MLA_ROPE]
        for i, v in enumerate((
            ckv[:mp].reshape(bp, seq, -1), kr[:mp].reshape(bp, seq, -1), s_p, conv_p,
            ckv[mp:].reshape(bs, dec, -1), kr[mp:].reshape(bs, dec, -1), s_s, conv_s, v_rows.reshape(bs, dec, -1),
        )):
            outs[i].append(v)

    return (x[:mp].reshape(bp, seq, d), x[mp:].reshape(bs, dec, d)) + tuple(jnp.stack(v, axis=0) for v in outs)
```

```python
import functools
import math

import jax
import jax.numpy as jnp
from jax import lax
from jax.experimental import pallas as pl
from jax.experimental.pallas import tpu as pltpu

F32 = jnp.float32
BF16 = jnp.bfloat16

D_MODEL = 2048
MLA_HEADS = 8
MLA_Q_RANK = 512
MLA_KV_RANK = 256
MLA_NOPE = 128
MLA_ROPE = 64
MLA_V = 128
ROPE_THETA = 10000.0
GDN_HEADS = 8
GDN_DK = 128
GDN_DV = 128
GDN_CONV = 4
GDN_CHUNK = 64
GDN_QKV = 2 * GDN_HEADS * GDN_DK + GDN_HEADS * GDN_DV
GDN_W = GDN_HEADS * GDN_DV
GMLP_GROUPS = 8
GMLP_GROUP_W = 128
GMLP_CHUNK = 128
GMLP_W = GMLP_GROUPS * GMLP_GROUP_W
N_BRANCH = 3
FFN_HIDDEN = -(-8 * D_MODEL // (3 * 256)) * 256
EPS = 1e-6

LANES = 128
VMEM_LIMIT = 56 * 1024 * 1024

Z_GATE = 0
Z_QKV = Z_GATE + N_BRANCH * D_MODEL
Z_U = Z_QKV + GDN_QKV
Z_V = Z_U + GMLP_W
Z_Z = Z_V + GMLP_W
Z_QLAT = Z_Z + GDN_W
Z_KV = Z_QLAT + MLA_Q_RANK
Z_SEG = Z_KV + MLA_KV_RANK
Z_COLS = 13312
SEG_A = MLA_ROPE
SEG_B = MLA_ROPE + GDN_HEADS

NT = (((1,), (1,)), ((), ()))
TN = (((0,), (0,)), ((), ()))


def _pick(n, cands):
    for c in cands:
        if n % c == 0:
            return c
    raise ValueError(f"no block size in {cands} divides {n}")


def _params(*sem):
    return pltpu.CompilerParams(dimension_semantics=sem, vmem_limit_bytes=VMEM_LIMIT)


def _rms(x, n=None):
    n = x.shape[-1] if n is None else n
    return x * lax.rsqrt(jnp.sum(x * x, axis=-1, keepdims=True) * (1.0 / n) + EPS)


def _silu(x):
    return x * jax.nn.sigmoid(x)


def _gelu(x):
    return 0.5 * x * (1.0 + lax.erf(x * (1.0 / math.sqrt(2.0))))


def _rope128(y, cs, sg):
    lane = lax.broadcasted_iota(jnp.int32, y.shape, 1)
    swapped = jnp.where((lane & 63) < 32, pltpu.roll(y, 96, 1), pltpu.roll(y, 32, 1))
    return y * cs + swapped * sg


def _w_spec(k, tn, layer, shift=0):
    return pl.BlockSpec((None, k, tn), lambda i, j: (layer, 0, j + shift))


def _norm_rows(x_ref, g_ref, h_sc):
    @pl.when(pl.program_id(1) == 0)
    def _():
        h_sc[...] = (_rms(x_ref[...]) * g_ref[...]).astype(h_sc.dtype)


def _norm_mm_kernel(x_ref, g_ref, w_ref, o_ref, h_sc):
    _norm_rows(x_ref, g_ref, h_sc)
    o_ref[...] = jnp.dot(h_sc[...], w_ref[...], preferred_element_type=F32)


def norm_matmul(x, g, w, layer, name):
    m, k = x.shape
    n = w.shape[2]
    tm = _pick(m, (1024, 512, 256, 128))
    tn = 512
    return pl.pallas_call(
        _norm_mm_kernel,
        grid=(m // tm, n // tn),
        in_specs=[pl.BlockSpec((tm, k), lambda i, j: (i, 0)), pl.BlockSpec((1, k), lambda i, j: (0, 0)), _w_spec(k, tn, layer)],
        out_specs=pl.BlockSpec((tm, tn), lambda i, j: (i, j)),
        out_shape=jax.ShapeDtypeStruct((m, n), F32),
        scratch_shapes=[pltpu.VMEM((tm, k), BF16)],
        compiler_params=_params("parallel", "arbitrary"),
        name=name,
    )(x, g.reshape(1, k), w)


def _mm_res_kernel(a_ref, w_ref, x_ref, o_ref):
    o_ref[...] = x_ref[...] + jnp.dot(a_ref[...], w_ref[...], preferred_element_type=F32)


def matmul_residual(a, w, layer, residual, tm_cands, name):
    m, k = a.shape
    n = w.shape[2]
    tm = _pick(m, tm_cands)
    tn = 512
    return pl.pallas_call(
        _mm_res_kernel,
        grid=(m // tm, n // tn),
        in_specs=[pl.BlockSpec((tm, k), lambda i, j: (i, 0)), _w_spec(k, tn, layer), pl.BlockSpec((tm, tn), lambda i, j: (i, j))],
        out_specs=pl.BlockSpec((tm, tn), lambda i, j: (i, j)),
        out_shape=jax.ShapeDtypeStruct((m, n), F32),
        compiler_params=_params("parallel", "arbitrary"),
        name=name,
    )(a, w, residual)


def _swiglu_kernel(x_ref, g_ref, wg_ref, wu_ref, o_ref, h_sc):
    _norm_rows(x_ref, g_ref, h_sc)
    h = h_sc[...]
    gate = jnp.dot(h, wg_ref[...], preferred_element_type=F32)
    up = jnp.dot(h, wu_ref[...], preferred_element_type=F32)
    o_ref[...] = (_silu(gate) * up).astype(o_ref.dtype)


def swiglu_up(x, g, w_gu, layer):
    m, k = x.shape
    hid = w_gu.shape[2] // 2
    tm = _pick(m, (1024, 512, 256, 128))
    tn = 512
    nb = hid // tn
    return pl.pallas_call(
        _swiglu_kernel,
        grid=(m // tm, nb),
        in_specs=[
            pl.BlockSpec((tm, k), lambda i, j: (i, 0)), pl.BlockSpec((1, k), lambda i, j: (0, 0)),
            _w_spec(k, tn, layer), _w_spec(k, tn, layer, shift=nb),
        ],
        out_specs=pl.BlockSpec((tm, tn), lambda i, j: (i, j)),
        out_shape=jax.ShapeDtypeStruct((m, hid), BF16),
        scratch_shapes=[pltpu.VMEM((tm, k), BF16)],
        compiler_params=_params("parallel", "arbitrary"),
        name="swiglu_up",
    )(x, g.reshape(1, k), w_gu, w_gu)


def _merge_kernel(*refs, prompt_blocks):
    a_prompt, a_sample, w_refs, g_refs, o_ref = refs[0:3], refs[3:6], refs[6:9], refs[9:12], refs[12]
    i = pl.program_id(0)

    def run(a_refs):
        acc = None
        for a_ref, w_ref, g_ref in zip(a_refs, w_refs, g_refs):
            y = jnp.dot(a_ref[...].astype(BF16), w_ref[...], preferred_element_type=F32)
            y = jax.nn.sigmoid(g_ref[...]) * y
            acc = y if acc is None else acc + y
        o_ref[...] = acc.astype(o_ref.dtype)

    @pl.when(i < prompt_blocks)
    def _():
        run(a_prompt)

    @pl.when(i >= prompt_blocks)
    def _():
        run(a_sample)


def gated_merge(branches_prompt, branches_sample, weights, layer, z):
    mp, k = branches_prompt[0].shape
    ms = branches_sample[0].shape[0]
    n = weights[0].shape[2]
    tm = _pick(math.gcd(mp, ms), (1024, 512, 256, 128))
    tn = 256
    nb = n // tn
    npb = mp // tm
    p_spec = pl.BlockSpec((tm, k), lambda i, j: (jnp.minimum(i, npb - 1), 0))
    s_spec = pl.BlockSpec((tm, k), lambda i, j: (jnp.maximum(i - npb, 0), 0))
    w_spec = _w_spec(k, tn, layer)
    g_specs = [pl.BlockSpec((tm, tn), functools.partial(lambda i, j, b: (i, Z_GATE // tn + b * nb + j), b=b)) for b in range(N_BRANCH)]
    return pl.pallas_call(
        functools.partial(_merge_kernel, prompt_blocks=npb),
        grid=((mp + ms) // tm, nb),
        in_specs=[p_spec] * 3 + [s_spec] * 3 + [w_spec] * 3 + g_specs,
        out_specs=pl.BlockSpec((tm, tn), lambda i, j: (i, j)),
        out_shape=jax.ShapeDtypeStruct((mp + ms, n), BF16),
        compiler_params=_params("parallel", "arbitrary"),
        name="gated_merge",
    )(*branches_prompt, *branches_sample, *weights, z, z, z)


def _mla_prep_kernel(ql_ref, kv_ref, seg_ref, cs_ref, sg_ref, gq_ref, gkv_ref, gkr_ref, qo_ref, ckv_ref, kr_ref):
    qo_ref[...] = (_rms(ql_ref[...]) * gq_ref[...]).astype(qo_ref.dtype)
    ckv_ref[...] = _rms(kv_ref[...]) * gkv_ref[...]
    seg = seg_ref[...]
    lane = lax.broadcasted_iota(jnp.int32, seg.shape, 1)
    x = jnp.where(lane < MLA_ROPE, seg, 0.0)
    kr_ref[...] = _rope128(_rms(x, MLA_ROPE) * gkr_ref[...], cs_ref[...], sg_ref[...])


def mla_prep(z, cs, sg, g_q, g_kv, g_kr_pad):
    m = z.shape[0]
    tm = _pick(m, (512, 256, 128))
    row = lambda w, c: pl.BlockSpec((tm, w), lambda i: (i, c))
    par = lambda w: pl.BlockSpec((1, w), lambda i: (0, 0))
    return pl.pallas_call(
        _mla_prep_kernel,
        grid=(m // tm,),
        in_specs=[
            row(MLA_Q_RANK, Z_QLAT // MLA_Q_RANK), row(MLA_KV_RANK, Z_KV // MLA_KV_RANK), row(LANES, Z_SEG // LANES),
            row(LANES, 0), row(LANES, 0), par(MLA_Q_RANK), par(MLA_KV_RANK), par(LANES),
        ],
        out_specs=[row(MLA_Q_RANK, 0), row(MLA_KV_RANK, 0), row(LANES, 0)],
        out_shape=[
            jax.ShapeDtypeStruct((m, MLA_Q_RANK), BF16),
            jax.ShapeDtypeStruct((m, MLA_KV_RANK), F32),
            jax.ShapeDtypeStruct((m, LANES), F32),
        ],
        compiler_params=_params("parallel"),
        name="mla_prep",
    )(z, z, z, cs, sg, g_q.reshape(1, -1), g_kv.reshape(1, -1), g_kr_pad)


def _q_kernel(a_ref, w_ref, cs_ref, sg_ref, gn_ref, gr_ref, o_ref, *, scale):
    a = a_ref[...]
    cs, sg = cs_ref[...], sg_ref[...]
    hw = MLA_NOPE + LANES
    q = jnp.dot(a, w_ref[...], preferred_element_type=F32)
    for h in range(MLA_HEADS):
        qn = q[:, h * hw:h * hw + MLA_NOPE]
        o_ref[:, h * hw:h * hw + MLA_NOPE] = (_rms(qn) * gn_ref[...] * scale).astype(o_ref.dtype)
        qr = q[:, h * hw + MLA_NOPE:(h + 1) * hw]
        qr = _rope128(_rms(qr, MLA_ROPE) * gr_ref[...], cs, sg)
        o_ref[:, h * hw + MLA_NOPE:(h + 1) * hw] = (qr * scale).astype(o_ref.dtype)


def q_project(q_lat, w_uq_pad, cs, sg, g_qn, g_qr_pad):
    m, k = q_lat.shape
    n = w_uq_pad.shape[1]
    tm = _pick(m, (512, 256, 128))
    scale = (MLA_NOPE + MLA_ROPE) ** -0.5
    return pl.pallas_call(
        functools.partial(_q_kernel, scale=scale),
        grid=(m // tm,),
        in_specs=[
            pl.BlockSpec((tm, k), lambda i: (i, 0)), pl.BlockSpec((k, n), lambda i: (0, 0)),
            pl.BlockSpec((tm, LANES), lambda i: (i, 0)), pl.BlockSpec((tm, LANES), lambda i: (i, 0)),
            pl.BlockSpec((1, LANES), lambda i: (0, 0)), pl.BlockSpec((1, LANES), lambda i: (0, 0)),
        ],
        out_specs=pl.BlockSpec((tm, n), lambda i: (i, 0)),
        out_shape=jax.ShapeDtypeStruct((m, n), BF16),
        compiler_params=_params("parallel"),
        name="q_project",
    )(q_lat, w_uq_pad, cs, sg, g_qn.reshape(1, -1), g_qr_pad)


def _kv_kernel(a_ref, wk_ref, wv_ref, kr_ref, gk_ref, ko_ref, vo_ref):
    a = a_ref[...].astype(BF16)
    krb = kr_ref[...].astype(BF16)
    hw = MLA_NOPE + LANES
    kn_all = jnp.dot(a, wk_ref[...], preferred_element_type=F32)
    for h in range(MLA_HEADS):
        kn = kn_all[:, h * MLA_NOPE:(h + 1) * MLA_NOPE]
        ko_ref[:, h * hw:h * hw + MLA_NOPE] = (_rms(kn) * gk_ref[...]).astype(ko_ref.dtype)
        ko_ref[:, h * hw + MLA_NOPE:(h + 1) * hw] = krb
    vo_ref[...] = jnp.dot(a, wv_ref[...], preferred_element_type=F32).astype(vo_ref.dtype)


def kv_expand(ckv, kr_pad, w_uk, w_uv, g_kn, rows):
    tm = _pick(rows, (512, 256, 128))
    hw = MLA_NOPE + LANES
    return pl.pallas_call(
        _kv_kernel,
        grid=(rows // tm,),
        in_specs=[
            pl.BlockSpec((tm, MLA_KV_RANK), lambda i: (i, 0)),
            pl.BlockSpec(w_uk.shape, lambda i: (0, 0)), pl.BlockSpec(w_uv.shape, lambda i: (0, 0)),
            pl.BlockSpec((tm, LANES), lambda i: (i, 0)), pl.BlockSpec((1, MLA_NOPE), lambda i: (0, 0)),
        ],
        out_specs=[pl.BlockSpec((tm, MLA_HEADS * hw), lambda i: (i, 0)), pl.BlockSpec((tm, MLA_HEADS * MLA_V), lambda i: (i, 0))],
        out_shape=[jax.ShapeDtypeStruct((rows, MLA_HEADS * hw), BF16), jax.ShapeDtypeStruct((rows, MLA_HEADS * MLA_V), BF16)],
        compiler_params=_params("parallel"),
        name="kv_expand",
    )(ckv, w_uk, w_uv, kr_pad, g_kn.reshape(1, -1))


def _flash_kernel(q_ref, k_ref, v_ref, o_ref, *, tq, nq, heads):
    qi = pl.program_id(2)
    hw = MLA_NOPE + LANES
    for i in range(nq):
        @pl.when(qi == i)
        def _(i=i):
            kend = (i + 1) * tq
            s = [lax.dot_general(q_ref[:, h * hw:(h + 1) * hw], k_ref[0:kend, h * hw:(h + 1) * hw], NT, preferred_element_type=F32)
                 for h in range(heads)]
            row = i * tq + lax.broadcasted_iota(jnp.int32, s[0].shape, 0)
            col = lax.broadcasted_iota(jnp.int32, s[0].shape, 1)
            p, l = [], []
            for h in range(heads):
                sh = jnp.where(col <= row, s[h], -jnp.inf)
                ph = jnp.exp(sh - jnp.max(sh, axis=-1, keepdims=True))
                l.append(jnp.sum(ph, axis=-1, keepdims=True))
                p.append(ph.astype(BF16))
            o = [jnp.dot(p[h], v_ref[0:kend, h * MLA_V:(h + 1) * MLA_V], preferred_element_type=F32) for h in range(heads)]
            for h in range(heads):
                o_ref[:, h * MLA_V:(h + 1) * MLA_V] = (o[h] / l[h]).astype(o_ref.dtype)


def mla_prompt_attention(qcat, kcat, v, batch, seq):
    tq = _pick(seq, (256, 128))
    nq = seq // tq
    heads = 2
    hw = heads * (MLA_NOPE + LANES)
    vw = heads * MLA_V
    return pl.pallas_call(
        functools.partial(_flash_kernel, tq=tq, nq=nq, heads=heads),
        grid=(batch, MLA_HEADS // heads, nq),
        in_specs=[
            pl.BlockSpec((tq, hw), lambda b, h, i: (b * nq + i, h)),
            pl.BlockSpec((seq, hw), lambda b, h, i: (b, h)),
            pl.BlockSpec((seq, vw), lambda b, h, i: (b, h)),
        ],
        out_specs=pl.BlockSpec((tq, vw), lambda b, h, i: (b * nq + i, h)),
        out_shape=jax.ShapeDtypeStruct((batch * seq, MLA_HEADS * MLA_V), BF16),
        compiler_params=_params("parallel", "parallel", "arbitrary"),
        name="mla_prompt_attention",
    )(qcat, kcat, v)


def _absorb_kernel(q_ref, w_ref, g_ref, o_ref):
    q = q_ref[:, 0:MLA_NOPE].astype(F32) * g_ref[...]
    o_ref[0] = jnp.dot(q.astype(BF16), w_ref[0], preferred_element_type=F32)


def absorb_queries(qcat, w_ukt, g_kn, row0, rows):
    hw = MLA_NOPE + LANES
    return pl.pallas_call(
        _absorb_kernel,
        grid=(MLA_HEADS,),
        in_specs=[
            pl.BlockSpec((rows, hw), lambda h: (row0 // rows, h)),
            pl.BlockSpec((1, MLA_NOPE, MLA_KV_RANK), lambda h: (h, 0, 0)),
            pl.BlockSpec((1, MLA_NOPE), lambda h: (0, 0)),
        ],
        out_specs=pl.BlockSpec((1, rows, MLA_KV_RANK), lambda h: (h, 0, 0)),
        out_shape=jax.ShapeDtypeStruct((MLA_HEADS, rows, MLA_KV_RANK), F32),
        compiler_params=_params("parallel"),
        name="absorb_queries",
    )(qcat, w_ukt, g_kn.reshape(1, -1))


def _unabsorb_kernel(a_ref, w_ref, o_ref):
    o_ref[...] = jnp.dot(a_ref[0].astype(BF16), w_ref[0], preferred_element_type=F32).astype(o_ref.dtype)


def unabsorb_values(o_lat, w_uv_h):
    _, rows, r = o_lat.shape
    return pl.pallas_call(
        _unabsorb_kernel,
        grid=(MLA_HEADS,),
        in_specs=[pl.BlockSpec((1, rows, r), lambda h: (h, 0, 0)), pl.BlockSpec((1, r, MLA_V), lambda h: (h, 0, 0))],
        out_specs=pl.BlockSpec((rows, MLA_V), lambda h: (0, h)),
        out_shape=jax.ShapeDtypeStruct((rows, MLA_HEADS * MLA_V), BF16),
        compiler_params=_params("parallel"),
        name="unabsorb_values",
    )(o_lat, w_uv_h)


def _sattn_kernel(pt_ref, qa_ref, qr_ref, cn_ref, kn_ref, wk_ref, ckv_hbm, krt_hbm, o_ref,
                  ckv_buf, krt_buf, ckb, s_sc, sem, *, layer, tile_pages, n_pages, page, dec):
    b = pl.program_id(0)
    nb = pl.num_programs(0)
    slot = b % 2
    ntiles = n_pages // tile_pages
    tk = tile_pages * page

    def tile_copies(bb, j, sl):
        out = []
        for g in range(tile_pages):
            slot_page = j * tile_pages + g
            pg = pt_ref[bb * n_pages + slot_page]
            out.append(pltpu.make_async_copy(ckv_hbm.at[layer, pg], ckv_buf.at[sl, slot_page], sem.at[sl, 0]))
            out.append(pltpu.make_async_copy(krt_hbm.at[layer, pg], krt_buf.at[sl, slot_page], sem.at[sl, 1]))
        return out

    @pl.when(b == 0)
    def _():
        def first(j, carry):
            for cp in tile_copies(b, j, slot):
                cp.start()
            return carry
        lax.fori_loop(0, ntiles, first, 0)

    pltpu.make_async_copy(ckv_hbm.at[layer, pl.ds(0, n_pages)], ckv_buf.at[slot], sem.at[slot, 0]).wait()
    pltpu.make_async_copy(krt_hbm.at[layer, pl.ds(0, n_pages)], krt_buf.at[slot], sem.at[slot, 1]).wait()

    qa = qa_ref[0].astype(BF16)
    qr = qr_ref[0].astype(BF16)
    wk = wk_ref[...]

    def nope_scores(ck):
        kn = lax.dot_general(wk, ck, NT, preferred_element_type=F32)
        raw = lax.dot_general(qa, ck, NT, preferred_element_type=F32)
        inv = []
        for h in range(MLA_HEADS):
            kh = kn[h * MLA_NOPE:(h + 1) * MLA_NOPE]
            r = lax.rsqrt(jnp.sum(kh * kh, axis=0, keepdims=True) * (1.0 / MLA_NOPE) + EPS)
            inv.append(jnp.broadcast_to(r, (dec, r.shape[1])))
        return raw * jnp.concatenate(inv, axis=0)

    def tile(j, carry):
        @pl.when(b + 1 < nb)
        def _():
            for cp in tile_copies(b + 1, j, 1 - slot):
                cp.start()

        r0 = pl.multiple_of(j * tk, tk)
        p0 = j * tile_pages
        ck = ckv_buf[slot, pl.ds(p0, tile_pages)].reshape(tk, ckv_buf.shape[-1]).astype(BF16)
        ckb[pl.ds(r0, tk), :] = ck
        krt = jnp.concatenate([krt_buf[slot, p0 + g] for g in range(tile_pages)], axis=1).astype(BF16)
        rope = jnp.dot(qr, krt, preferred_element_type=F32)
        s_sc[j] = nope_scores(ck) + rope
        return carry
    lax.fori_loop(0, ntiles, tile, 0)

    pad = page - dec
    cnew = jnp.concatenate([cn_ref[...], jnp.zeros((pad, cn_ref.shape[1]), F32)], axis=0).astype(BF16)
    knew = jnp.concatenate([kn_ref[:, 0:MLA_ROPE], jnp.zeros((pad, MLA_ROPE), F32)], axis=0).astype(BF16)
    s_new = nope_scores(cnew) + lax.dot_general(qr, knew, NT, preferred_element_type=F32)
    qpos = lax.broadcasted_iota(jnp.int32, s_new.shape, 0) % dec
    kpos = lax.broadcasted_iota(jnp.int32, s_new.shape, 1)
    s_new = jnp.where(kpos <= qpos, s_new, -jnp.inf)

    m = jnp.max(s_new, axis=-1, keepdims=True)
    for j in range(ntiles):
        m = jnp.maximum(m, jnp.max(s_sc[j], axis=-1, keepdims=True))
    p = jnp.exp(s_new - m)
    l = jnp.sum(p, axis=-1, keepdims=True)
    acc = jnp.dot(p.astype(BF16), cnew, preferred_element_type=F32)
    for j in range(ntiles):
        p = jnp.exp(s_sc[j] - m)
        l = l + jnp.sum(p, axis=-1, keepdims=True)
        acc = acc + jnp.dot(p.astype(BF16), ckb[j * tk:(j + 1) * tk, :], preferred_element_type=F32)
    o_ref[0] = acc / l


def mla_sample_attention(q_abs, q_rope, ckv, kr_pad, w_ukt2, cache_ckv, cache_krope_t, page_table, layer, row0, dec):
    nb, nq, r = q_abs.shape
    n_pages = page_table.shape[1]
    page = cache_ckv.shape[2]
    tile_pages = _pick(n_pages, (8, 4, 2, 1))
    ntiles = n_pages // tile_pages
    tk = tile_pages * page
    grid_spec = pltpu.PrefetchScalarGridSpec(
        num_scalar_prefetch=1,
        grid=(nb,),
        in_specs=[
            pl.BlockSpec((1, nq, r), lambda b, pt: (b, 0, 0)),
            pl.BlockSpec((1, nq, MLA_ROPE), lambda b, pt: (b, 0, 0)),
            pl.BlockSpec((dec, r), lambda b, pt: (row0 // dec + b, 0)),
            pl.BlockSpec((dec, LANES), lambda b, pt: (row0 // dec + b, 0)),
            pl.BlockSpec(w_ukt2.shape, lambda b, pt: (0, 0)),
            pl.BlockSpec(memory_space=pl.ANY),
            pl.BlockSpec(memory_space=pl.ANY),
        ],
        out_specs=pl.BlockSpec((1, nq, r), lambda b, pt: (b, 0, 0)),
        scratch_shapes=[
            pltpu.VMEM((2, n_pages, page, r), F32),
            pltpu.VMEM((2, n_pages, MLA_ROPE, page), F32),
            pltpu.VMEM((n_pages * page, r), BF16),
            pltpu.VMEM((ntiles, nq, tk), F32),
            pltpu.SemaphoreType.DMA((2, 2)),
        ],
    )
    return pl.pallas_call(
        functools.partial(_sattn_kernel, layer=layer, tile_pages=tile_pages, n_pages=n_pages, page=page, dec=dec),
        grid_spec=grid_spec,
        out_shape=jax.ShapeDtypeStruct((nb, nq, r), F32),
        compiler_params=_params("arbitrary"),
        name="mla_sample_attention",
    )(page_table.reshape(-1), q_abs, q_rope, ckv, kr_pad, w_ukt2, cache_ckv, cache_krope_t)


def _chunk_masks(c):
    ii = lax.broadcasted_iota(jnp.int32, (c, c), 0)
    jj = lax.broadcasted_iota(jnp.int32, (c, c), 1)
    return jj <= ii, jj < ii


def _split_bf16(x, terms):
    parts = []
    for _ in range(terms):
        p = x.astype(BF16)
        parts.append(p)
        x = x - p.astype(F32)
    return parts


def _dot_split(a, b):
    ah, al = _split_bf16(a, 2)
    bh, bl = _split_bf16(b, 2)
    dot = functools.partial(jnp.dot, preferred_element_type=F32)
    return dot(ah, bh) + (dot(ah, bl) + dot(al, bh))


def _gdn_prep_kernel(x_ref, seg_ref, c0_ref, cw_ref, al_ref, dt_ref, qkv_ref, gb_ref, cn_ref, a_ref, ext, *, nsq, rows, chunk):
    t = pl.program_id(1)
    hk = GDN_HEADS * GDN_DK
    tail = GDN_CONV - 1
    tb = nsq * rows
    cw = cw_ref[...]
    for sq in range(nsq):
        @pl.when(t == 0)
        def _(sq=sq):
            ext[sq, 8 - tail:8, :] = c0_ref[sq]

        ext[sq, 8:8 + rows, :] = x_ref[sq * rows:(sq + 1) * rows, :]
        y = cw[0:1] * ext[sq, 5:5 + rows, :]
        for j in range(1, GDN_CONV):
            y = y + cw[j:j + 1] * ext[sq, 5 + j:5 + j + rows, :]
        new_tail = ext[sq, 8 + rows - tail:8 + rows, :]
        cn_ref[sq] = new_tail
        ext[sq, 8 - tail:8, :] = new_tail
        qkv_ref[sq * rows:(sq + 1) * rows, :] = _silu(y)

    seg = seg_ref[...]
    lane = lax.broadcasted_iota(jnp.int32, seg.shape, 1)
    beta = jax.nn.sigmoid(seg)
    g = -jnp.exp(al_ref[...]) * jax.nn.softplus(seg + dt_ref[...])
    ri = lax.broadcasted_iota(jnp.int32, (tb, tb), 0)
    ci = lax.broadcasted_iota(jnp.int32, (tb, tb), 1)
    cum = jnp.where((ri // chunk == ci // chunk) & (ci <= ri), 1.0, 0.0).astype(BF16)
    gcum = sum(jnp.dot(cum, p, preferred_element_type=F32) for p in reversed(_split_bf16(g, 3)))
    gb = jnp.where((lane >= SEG_A) & (lane < SEG_B), gcum, beta)
    gb_ref[...] = gb
    gcum_t = gcum.T

    _, strict = _chunk_masks(chunk)
    for h in range(GDN_HEADS):
        qh = qkv_ref[:, h * GDN_DK:(h + 1) * GDN_DK]
        qh = qh * lax.rsqrt(jnp.sum(qh * qh, axis=-1, keepdims=True) + EPS) * (GDN_DK ** -0.5)
        kh = qkv_ref[:, hk + h * GDN_DK:hk + (h + 1) * GDN_DK]
        kh = kh * lax.rsqrt(jnp.sum(kh * kh, axis=-1, keepdims=True) + EPS)
        qkv_ref[:, h * GDN_DK:(h + 1) * GDN_DK] = qh
        qkv_ref[:, hk + h * GDN_DK:hk + (h + 1) * GDN_DK] = kh
        kb = (kh * beta[:, SEG_B + h:SEG_B + h + 1]).astype(BF16)
        kh = kh.astype(BF16)
        for c in range(tb // chunk):
            r0, r1 = c * chunk, (c + 1) * chunk
            gc = gcum[r0:r1, SEG_A + h:SEG_A + h + 1]
            gr = gcum_t[SEG_A + h:SEG_A + h + 1, r0:r1]
            decay = jnp.exp(jnp.where(strict, gc - gr, 0.0))
            kk = lax.dot_general(kb[r0:r1], kh[r0:r1], NT, preferred_element_type=F32)
            a_ref[c, h] = jnp.where(strict, kk * decay, 0.0)


def gdn_prep(z, conv0, layer, conv_w, a_log_row, dt_row, row0, nseq, seqlen, nsq, rows, chunk):
    nt = seqlen // rows
    tb = nsq * rows
    nc = tb // chunk
    base = row0 // tb
    assert nsq == 1 or nt == 1
    return pl.pallas_call(
        functools.partial(_gdn_prep_kernel, nsq=nsq, rows=rows, chunk=chunk),
        grid=(nseq // nsq, nt),
        in_specs=[
            pl.BlockSpec((tb, GDN_QKV), lambda b, t: (base + b * nt + t, Z_QKV // GDN_QKV)),
            pl.BlockSpec((tb, LANES), lambda b, t: (base + b * nt + t, Z_SEG // LANES)),
            pl.BlockSpec((None, nsq, GDN_CONV - 1, GDN_QKV), lambda b, t: (layer, b, 0, 0)),
            pl.BlockSpec((GDN_CONV, GDN_QKV), lambda b, t: (0, 0)),
            pl.BlockSpec((1, LANES), lambda b, t: (0, 0)),
            pl.BlockSpec((1, LANES), lambda b, t: (0, 0)),
        ],
        out_specs=[
            pl.BlockSpec((tb, GDN_QKV), lambda b, t: (b * nt + t, 0)),
            pl.BlockSpec((tb, LANES), lambda b, t: (b * nt + t, 0)),
            pl.BlockSpec((nsq, GDN_CONV - 1, GDN_QKV), lambda b, t: (b, 0, 0)),
            pl.BlockSpec((nc, GDN_HEADS, chunk, chunk), lambda b, t: (b * nt + t, 0, 0, 0)),
        ],
        out_shape=[
            jax.ShapeDtypeStruct((nseq * seqlen, GDN_QKV), F32),
            jax.ShapeDtypeStruct((nseq * seqlen, LANES), F32),
            jax.ShapeDtypeStruct((nseq, GDN_CONV - 1, GDN_QKV), F32),
            jax.ShapeDtypeStruct((nseq * seqlen // chunk, GDN_HEADS, chunk, chunk), F32),
        ],
        scratch_shapes=[pltpu.VMEM((nsq, rows + 8, GDN_QKV), F32)],
        compiler_params=_params("parallel", "arbitrary"),
        name="gdn_prep",
    )(z, z, conv0, conv_w, a_log_row, dt_row)


def _trinv_kernel(a_ref, o_ref, t_sc, *, n, kb):
    i = pl.program_id(0)
    for b in range(n // kb):
        k0 = b * kb

        def body(j, acc, k0=k0):
            return acc + a_ref[0, j][None] * t_sc[j, k0:k0 + kb]

        acc = lax.fori_loop(k0, i, body, jnp.zeros((kb,) + t_sc.shape[2:], F32))
        kidx = k0 + lax.broadcasted_iota(jnp.int32, acc.shape, 0)
        row = jnp.where(kidx == i, 1.0, 0.0) - acc
        t_sc[i, k0:k0 + kb] = row
        o_ref[0, k0:k0 + kb] = row


def unit_lower_inverse(a):
    nsys, n, _ = a.shape
    tile = 8 * LANES
    npad = -(-nsys // tile) * tile
    at = jnp.transpose(a, (1, 2, 0))
    if npad != nsys:
        at = jnp.pad(at, ((0, 0), (0, 0), (0, npad - nsys)))
    at = at.reshape(n, n, npad // LANES, LANES)
    kb = min(n, 16)
    outs = []
    for s in range(npad // tile):
        blk = at[:, :, s * 8:(s + 1) * 8]
        outs.append(pl.pallas_call(
            functools.partial(_trinv_kernel, n=n, kb=kb),
            grid=(n,),
            in_specs=[pl.BlockSpec((1, n, 8, LANES), lambda i: (i, 0, 0, 0))],
            out_specs=pl.BlockSpec((1, n, 8, LANES), lambda i: (i, 0, 0, 0)),
            out_shape=jax.ShapeDtypeStruct((n, n, 8, LANES), F32),
            scratch_shapes=[pltpu.VMEM((n, n, 8, LANES), F32)],
            compiler_params=_params("arbitrary"),
            name="unit_lower_inverse",
        )(blk))
    t = outs[0] if len(outs) == 1 else jnp.concatenate(outs, axis=2)
    t = t.reshape(n, n, npad)[:, :, :nsys]
    return jnp.transpose(t, (2, 0, 1))


def _gdn_scan_kernel(qkv_ref, gb_ref, z_ref, ti_ref, s0_ref, gon_ref, o_ref, so_ref, s_sc, *, nsq, rows, chunk):
    t = pl.program_id(1)
    hk = GDN_HEADS * GDN_DK

    @pl.when(t == 0)
    def _():
        s_sc[...] = s0_ref[...]

    gb = gb_ref[...]
    gb_t = gb.T
    incl, _ = _chunk_masks(chunk)
    ncs = rows // chunk
    pairs = [(sq, h) for sq in range(nsq) for h in range(GDN_HEADS)]
    dot = functools.partial(jnp.dot, preferred_element_type=F32)
    for c in range(ncs):
        span = {(sq, h): (sq * rows + c * chunk, sq * rows + (c + 1) * chunk) for sq, h in pairs}
        st1, st2, st3, st4 = {}, {}, {}, {}
        for sq, h in pairs:
            r0, r1 = span[sq, h]
            gc = gb[r0:r1, SEG_A + h:SEG_A + h + 1]
            gr = gb_t[SEG_A + h:SEG_A + h + 1, r0:r1]
            beta = gb[r0:r1, SEG_B + h:SEG_B + h + 1]
            q = qkv_ref[r0:r1, h * GDN_DK:(h + 1) * GDN_DK]
            k = qkv_ref[r0:r1, hk + h * GDN_DK:hk + (h + 1) * GDN_DK]
            v = qkv_ref[r0:r1, 2 * hk + h * GDN_DV:2 * hk + (h + 1) * GDN_DV]
            egc = jnp.exp(gc)
            glast = gc[chunk - 1:chunk, :]
            th, tl = _split_bf16(ti_ref[sq * ncs + c, h], 2)
            rhs = jnp.concatenate([v * beta, k * (beta * egc)], axis=1)
            rh, rl = _split_bf16(rhs, 2)
            sol = (dot(th, rh), dot(th, rl), dot(tl, rh))
            qk = lax.dot_general(q.astype(BF16), k.astype(BF16), NT, preferred_element_type=F32)
            decay = jnp.exp(jnp.where(incl, gc - gr, 0.0))
            k_tail = (k * jnp.exp(glast - gc)).astype(BF16)
            st1[sq, h] = (sol, qk, decay, q * egc, k_tail, jnp.exp(glast))
        for sq, h in pairs:
            sol, qk, decay, qe, k_tail, dec_last = st1[sq, h]
            uw = sol[0] + (sol[1] + sol[2])
            wq = jnp.concatenate([uw[:, GDN_DV:], qe], axis=0).astype(BF16)
            attn = jnp.where(incl, qk * decay, 0.0).astype(BF16)
            st2[sq, h] = (uw[:, :GDN_DV], wq, attn, k_tail, dec_last)
        for sq, h in pairs:
            u, wq, attn, k_tail, dec_last = st2[sq, h]
            s_h = s_sc[sq, h]
            st3[sq, h] = (u, dot(wq, s_h.astype(BF16)), attn, k_tail, s_h * dec_last)
        for sq, h in pairs:
            u, ws, attn, k_tail, s_dec = st3[sq, h]
            v_new_b = (u - ws[:chunk]).astype(BF16)
            st4[sq, h] = ws[chunk:] + dot(attn, v_new_b)
            s_sc[sq, h] = s_dec + lax.dot_general(k_tail, v_new_b, TN, preferred_element_type=F32)
        for sq, h in pairs:
            r0, r1 = span[sq, h]
            zg = z_ref[r0:r1, h * GDN_DV:(h + 1) * GDN_DV]
            o_ref[r0:r1, h * GDN_DV:(h + 1) * GDN_DV] = (_rms(st4[sq, h]) * gon_ref[...] * _silu(zg)).astype(o_ref.dtype)

    @pl.when(t == pl.num_programs(1) - 1)
    def _():
        so_ref[...] = s_sc[...]


def gdn_scan(qkv, gb, z, tinv, s0, layer, g_onorm, row0, nseq, seqlen, nsq, rows, chunk, out_dtype):
    nt = seqlen // rows
    tb = nsq * rows
    nc = tb // chunk
    base = row0 // tb
    state = (nsq, GDN_HEADS, GDN_DK, GDN_DV)
    return pl.pallas_call(
        functools.partial(_gdn_scan_kernel, nsq=nsq, rows=rows, chunk=chunk),
        grid=(nseq // nsq, nt),
        in_specs=[
            pl.BlockSpec((tb, GDN_QKV), lambda b, t: (b * nt + t, 0)),
            pl.BlockSpec((tb, LANES), lambda b, t: (b * nt + t, 0)),
            pl.BlockSpec((tb, GDN_W), lambda b, t: (base + b * nt + t, Z_Z // GDN_W)),
            pl.BlockSpec((nc, GDN_HEADS, chunk, chunk), lambda b, t: (b * nt + t, 0, 0, 0)),
            pl.BlockSpec((None,) + state, lambda b, t: (layer, b, 0, 0, 0)),
            pl.BlockSpec((1, GDN_DV), lambda b, t: (0, 0)),
        ],
        out_specs=[
            pl.BlockSpec((tb, GDN_W), lambda b, t: (b * nt + t, 0)),
            pl.BlockSpec(state, lambda b, t: (b, 0, 0, 0)),
        ],
        out_shape=[
            jax.ShapeDtypeStruct((nseq * seqlen, GDN_W), out_dtype),
            jax.ShapeDtypeStruct((nseq, GDN_HEADS, GDN_DK, GDN_DV), F32),
        ],
        scratch_shapes=[pltpu.VMEM(state, F32)],
        compiler_params=_params("parallel", "arbitrary"),
        name="gdn_scan",
    )(qkv, gb, z, tinv, s0, g_onorm.reshape(1, -1))


def gdn_branch(z, conv0, s0, layer, conv_w, a_log_row, dt_row, g_onorm, row0, nseq, seqlen):
    chunk = min(GDN_CHUNK, seqlen)
    rows = _pick(seqlen, (256, 128, 64, 8))
    nsq = _pick(nseq, (8, 4, 2, 1)) if rows == seqlen and rows < GDN_CHUNK else 1
    out_dtype = BF16 if chunk % 16 == 0 else F32
    qkv, gb, conv_new, a = gdn_prep(z, conv0, layer, conv_w, a_log_row, dt_row, row0, nseq, seqlen, nsq, rows, chunk)
    tinv = unit_lower_inverse(a.reshape(-1, chunk, chunk)).reshape(a.shape)
    o, s_new = gdn_scan(qkv, gb, z, tinv, s0, layer, g_onorm, row0, nseq, seqlen, nsq, rows, chunk, out_dtype)
    return o, conv_new, s_new


def _layer_norm(v, g, b):
    mu = jnp.mean(v, axis=-1, keepdims=True)
    d = v - mu
    return d * lax.rsqrt(jnp.mean(d * d, axis=-1, keepdims=True) + EPS) * g + b


def _gmlp_prompt_kernel(u_ref, v_ref, lg_ref, lb_ref, w_ref, bias_ref, o_ref, *, rows):
    vn = _layer_norm(_gelu(v_ref[...]), lg_ref[...], lb_ref[...]).astype(BF16)
    u = _gelu(u_ref[...])
    tril, _ = _chunk_masks(GMLP_CHUNK)
    bias = bias_ref[...]
    blocks = [(g, c) for g in range(GMLP_GROUPS) for c in range(rows // GMLP_CHUNK)]
    wg = [jnp.where(tril, w_ref[g], 0.0).astype(BF16) for g in range(GMLP_GROUPS)]
    s = {}
    for g, c in blocks:
        s[g, c] = jnp.dot(wg[g], vn[c * GMLP_CHUNK:(c + 1) * GMLP_CHUNK, g * GMLP_GROUP_W:(g + 1) * GMLP_GROUP_W], preferred_element_type=F32)
    for g, c in blocks:
        r0, r1 = c * GMLP_CHUNK, (c + 1) * GMLP_CHUNK
        c0, c1 = g * GMLP_GROUP_W, (g + 1) * GMLP_GROUP_W
        o_ref[r0:r1, c0:c1] = (u[r0:r1, c0:c1] * (s[g, c] + bias[:, c0:c1])).astype(o_ref.dtype)


def gmlp_prompt(z, ln_g, ln_b, w_s, bias_full, nrows):
    rows = _pick(nrows, (512, 256, 128))
    row = lambda cb: pl.BlockSpec((rows, GMLP_W), lambda i: (i, cb))
    par = lambda: pl.BlockSpec((1, GMLP_W), lambda i: (0, 0))
    return pl.pallas_call(
        functools.partial(_gmlp_prompt_kernel, rows=rows),
        grid=(nrows // rows,),
        in_specs=[
            row(Z_U // GMLP_W), row(Z_V // GMLP_W), par(), par(),
            pl.BlockSpec(w_s.shape, lambda i: (0, 0, 0)),
            pl.BlockSpec(bias_full.shape, lambda i: (0, 0)),
        ],
        out_specs=row(0),
        out_shape=jax.ShapeDtypeStruct((nrows, GMLP_W), BF16),
        compiler_params=_params("parallel"),
        name="gmlp_prompt",
    )(z, z, ln_g.reshape(1, -1), ln_b.reshape(1, -1), w_s, bias_full)


def _gmlp_sample_kernel(u_ref, v_ref, lg_ref, lb_ref, wt_ref, b_ref, o_ref, vr_ref, *, rows, dec):
    vn = _layer_norm(_gelu(v_ref[...]), lg_ref[...], lb_ref[...])
    vr_ref[...] = vn
    v3 = vn.reshape(rows // dec, dec, GMLP_W)
    tpos = lax.broadcasted_iota(jnp.int32, (dec, GMLP_W), 0)
    s3 = jnp.broadcast_to(b_ref[...][None], v3.shape)
    for j in range(dec):
        wj = jnp.where(tpos >= j, wt_ref[j], 0.0)
        s3 = s3 + wj[None] * v3[:, j:j + 1, :]
    u3 = _gelu(u_ref[...]).reshape(rows // dec, dec, GMLP_W)
    o_ref[...] = (u3 * s3).reshape(rows, GMLP_W).astype(o_ref.dtype)


def gmlp_sample(z, ln_g, ln_b, wt, b_dec, row0, nrows, dec):
    rows = _pick(nrows, (256, 128))
    base = row0 // rows
    row = lambda cb, off: pl.BlockSpec((rows, GMLP_W), lambda i: (off + i, cb))
    par = lambda: pl.BlockSpec((1, GMLP_W), lambda i: (0, 0))
    return pl.pallas_call(
        functools.partial(_gmlp_sample_kernel, rows=rows, dec=dec),
        grid=(nrows // rows,),
        in_specs=[
            row(Z_U // GMLP_W, base), row(Z_V // GMLP_W, base), par(), par(),
            pl.BlockSpec(wt.shape, lambda i: (0, 0, 0)),
            pl.BlockSpec(b_dec.shape, lambda i: (0, 0)),
        ],
        out_specs=[row(0, 0), row(0, 0)],
        out_shape=[jax.ShapeDtypeStruct((nrows, GMLP_W), BF16), jax.ShapeDtypeStruct((nrows, GMLP_W), F32)],
        compiler_params=_params("parallel"),
        name="gmlp_sample",
    )(z, z, ln_g.reshape(1, -1), ln_b.reshape(1, -1), wt, b_dec)


def _rope_tables(pos):
    inv = ROPE_THETA ** (-jnp.arange(0, MLA_ROPE, 2, dtype=F32) / MLA_ROPE)
    ang = pos.astype(F32)[:, None] * inv[None, :]
    cos, sin = jnp.cos(ang), jnp.sin(ang)
    zero = jnp.zeros((pos.shape[0], LANES - MLA_ROPE), F32)
    return jnp.concatenate([cos, cos, zero], axis=1), jnp.concatenate([-sin, sin, zero], axis=1)


def _pad_lanes(v, offset=0):
    return jnp.zeros((1, LANES), F32).at[0, offset:offset + v.shape[0]].set(v)


def kernel(x_prompt, x_sample, cache_ckv, cache_krope, state_gdn, state_conv, page_table, norm_mix_g, w_in, mla_q_norm_g, mla_w_uq, mla_qn_g, mla_qr_g, mla_kv_norm_g, mla_kr_g, mla_w_uk, mla_kn_g, mla_w_uv, gdn_conv_w, gdn_a_log, gdn_dt_bias, gdn_o_norm_g, gmlp_ln_g, gmlp_ln_b, gmlp_w_s, gmlp_b_s, w_br_mla, w_br_gdn, w_br_gmlp, w_o, norm_ffn_g, ffn_w_gu, ffn_w_down):
    bp, seq, d = x_prompt.shape
    bs, dec, _ = x_sample.shape
    depth = w_in.shape[0]
    mp, ms = bp * seq, bs * dec
    n_past = page_table.shape[1] * cache_ckv.shape[2]
    assert d == D_MODEL and dec == 8 and mp % ms == 0 and seq % GMLP_CHUNK == 0 and seq % GDN_CHUNK == 0

    x = jnp.concatenate([x_prompt.reshape(mp, d), x_sample.reshape(ms, d)], axis=0)
    pos = jnp.concatenate([jnp.tile(jnp.arange(seq), bp), jnp.tile(n_past + jnp.arange(dec), bs)])
    cs, sg = _rope_tables(pos)
    cache_krope_t = jnp.swapaxes(cache_krope, 2, 3)
    conv0_p = jnp.zeros((1, bp, GDN_CONV - 1, GDN_QKV), F32)
    s0_p = jnp.zeros((1, bp, GDN_HEADS, GDN_DK, GDN_DV), F32)

    o = [0]
    for n in (MLA_Q_RANK, MLA_KV_RANK, MLA_ROPE, GDN_QKV, GDN_W, GDN_HEADS, GDN_HEADS, GMLP_W, GMLP_W, N_BRANCH * D_MODEL):
        o.append(o[-1] + n)
    w_in_p = jnp.concatenate([
        w_in[:, :, o[9]:o[10]], w_in[:, :, o[3]:o[4]], w_in[:, :, o[7]:o[8]], w_in[:, :, o[8]:o[9]], w_in[:, :, o[4]:o[5]],
        w_in[:, :, o[0]:o[1]], w_in[:, :, o[1]:o[2]], w_in[:, :, o[2]:o[3]], w_in[:, :, o[5]:o[7]],
        jnp.zeros((depth, d, Z_COLS - Z_SEG - MLA_ROPE - 2 * GDN_HEADS), F32),
    ], axis=2).astype(BF16)
    w_br = (w_br_mla.astype(BF16), w_br_gdn.astype(BF16), w_br_gmlp.astype(BF16))
    w_o_b = w_o.astype(BF16)
    w_gu_b = ffn_w_gu.astype(BF16)
    w_down_b = ffn_w_down.astype(BF16)

    outs = [[] for _ in range(9)]
    for l in range(depth):
        w_uq = mla_w_uq[l].reshape(MLA_Q_RANK, MLA_HEADS, MLA_NOPE + MLA_ROPE)
        w_uq = jnp.concatenate([w_uq, jnp.zeros((MLA_Q_RANK, MLA_HEADS, LANES - MLA_ROPE), F32)], axis=-1)
        w_uq = w_uq.reshape(MLA_Q_RANK, -1).astype(BF16)
        w_uk = mla_w_uk[l].reshape(MLA_KV_RANK, -1).astype(BF16)
        w_uv = mla_w_uv[l].reshape(MLA_KV_RANK, -1).astype(BF16)
        w_ukt = jnp.transpose(mla_w_uk[l], (1, 2, 0)).astype(BF16)
        w_uv_h = jnp.transpose(mla_w_uv[l], (1, 0, 2)).astype(BF16)
        g_kr_pad = _pad_lanes(mla_kr_g[l])
        g_qr_pad = _pad_lanes(mla_qr_g[l])
        a_log_row = _pad_lanes(gdn_a_log[l], SEG_A)
        dt_row = _pad_lanes(gdn_dt_bias[l], SEG_A)
        bias_full = jnp.repeat(gmlp_b_s[l].T, GMLP_GROUP_W, axis=1)
        wt_dec = jnp.repeat(jnp.transpose(gmlp_w_s[l][:, :dec, :dec], (2, 1, 0)), GMLP_GROUP_W, axis=2)

        z = norm_matmul(x, norm_mix_g[l], w_in_p, l, name="in_proj")

        q_lat, ckv, kr_pad = mla_prep(z, cs, sg, mla_q_norm_g[l], mla_kv_norm_g[l], g_kr_pad)
        qcat = q_project(q_lat, w_uq, cs, sg, mla_qn_g[l], g_qr_pad)
        kcat, vexp = kv_expand(ckv, kr_pad, w_uk, w_uv, mla_kn_g[l], mp)
        o_mla_p = mla_prompt_attention(qcat, kcat, vexp, bp, seq)
        q_abs = absorb_queries(qcat, w_ukt, mla_kn_g[l], mp, ms)
        q_abs = jnp.transpose(q_abs.reshape(MLA_HEADS, bs, dec, MLA_KV_RANK), (1, 0, 2, 3)).reshape(bs, MLA_HEADS * dec, MLA_KV_RANK)
        q_rope = qcat[mp:].reshape(bs, dec, MLA_HEADS, MLA_NOPE + LANES)[..., MLA_NOPE:MLA_NOPE + MLA_ROPE]
        q_rope = jnp.transpose(q_rope, (0, 2, 1, 3)).reshape(bs, MLA_HEADS * dec, MLA_ROPE).astype(F32)
        o_lat = mla_sample_attention(q_abs, q_rope, ckv, kr_pad, w_ukt.reshape(-1, MLA_KV_RANK), cache_ckv, cache_krope_t, page_table, l, mp, dec)
        o_lat = jnp.transpose(o_lat.reshape(bs, MLA_HEADS, dec, MLA_KV_RANK), (1, 0, 2, 3)).reshape(MLA_HEADS, ms, MLA_KV_RANK)
        o_mla_s = unabsorb_values(o_lat, w_uv_h)

        o_gdn_p, conv_p, s_p = gdn_branch(z, conv0_p, s0_p, 0, gdn_conv_w[l], a_log_row, dt_row, gdn_o_norm_g[l], 0, bp, seq)
        o_gdn_s, conv_s, s_s = gdn_branch(z, state_conv, state_gdn, l, gdn_conv_w[l], a_log_row, dt_row, gdn_o_norm_g[l], mp, bs, dec)

        o_gmlp_p = gmlp_prompt(z, gmlp_ln_g[l], gmlp_ln_b[l], gmlp_w_s[l], bias_full, mp)
        o_gmlp_s, v_rows = gmlp_sample(z, gmlp_ln_g[l], gmlp_ln_b[l], wt_dec, bias_full[:dec], mp, ms, dec)

        merged = gated_merge((o_mla_p, o_gdn_p, o_gmlp_p), (o_mla_s, o_gdn_s, o_gmlp_s), w_br, l, z)
        x = matmul_residual(merged, w_o_b, l, x, (1024, 512, 256, 128), name="out_proj")
        act = swiglu_up(x, norm_ffn_g[l], w_gu_b, l)
        x = matmul_residual(act, w_down_b, l, x, (512, 256, 128), name="ffn_down")

        kr = kr_pad[:, :MLA_ROPE]
        for i, v in enumerate((
            ckv[:mp].reshape(bp, seq, -1), kr[:mp].reshape(bp, seq, -1), s_p, conv_p,
            ckv[mp:].reshape(bs, dec, -1), kr[mp:].reshape(bs, dec, -1), s_s, conv_s, v_rows.reshape(bs, dec, -1),
        )):
            outs[i].append(v)

    return (x[:mp].reshape(bp, seq, d), x[mp:].reshape(bs, dec, d)) + tuple(jnp.stack(v, axis=0) for v in outs)
```

```python
import functools
import math

import jax
import jax.numpy as jnp
from jax import lax
from jax.experimental import pallas as pl
from jax.experimental.pallas import tpu as pltpu

F32 = jnp.float32
BF16 = jnp.bfloat16

D_MODEL = 2048
MLA_HEADS = 8
MLA_Q_RANK = 512
MLA_KV_RANK = 256
MLA_NOPE = 128
MLA_ROPE = 64
MLA_V = 128
ROPE_THETA = 10000.0
GDN_HEADS = 8
GDN_DK = 128
GDN_DV = 128
GDN_CONV = 4
GDN_CHUNK = 64
GDN_QKV = 2 * GDN_HEADS * GDN_DK + GDN_HEADS * GDN_DV
GDN_W = GDN_HEADS * GDN_DV
GMLP_GROUPS = 8
GMLP_GROUP_W = 128
GMLP_CHUNK = 128
GMLP_W = GMLP_GROUPS * GMLP_GROUP_W
N_BRANCH = 3
FFN_HIDDEN = -(-8 * D_MODEL // (3 * 256)) * 256
EPS = 1e-6

LANES = 128
VMEM_LIMIT = 56 * 1024 * 1024

Z_GATE = 0
Z_QKV = Z_GATE + N_BRANCH * D_MODEL
Z_U = Z_QKV + GDN_QKV
Z_V = Z_U + GMLP_W
Z_Z = Z_V + GMLP_W
Z_QLAT = Z_Z + GDN_W
Z_KV = Z_QLAT + MLA_Q_RANK
Z_SEG = Z_KV + MLA_KV_RANK
Z_COLS = 13312
SEG_A = MLA_ROPE
SEG_B = MLA_ROPE + GDN_HEADS

NT = (((1,), (1,)), ((), ()))
TN = (((0,), (0,)), ((), ()))


def _pick(n, cands):
    for c in cands:
        if n % c == 0:
            return c
    raise ValueError(f"no block size in {cands} divides {n}")


def _params(*sem):
    return pltpu.CompilerParams(dimension_semantics=sem, vmem_limit_bytes=VMEM_LIMIT)


def _rms(x, n=None):
    n = x.shape[-1] if n is None else n
    return x * lax.rsqrt(jnp.sum(x * x, axis=-1, keepdims=True) * (1.0 / n) + EPS)


def _silu(x):
    return x * jax.nn.sigmoid(x)


def _gelu(x):
    return 0.5 * x * (1.0 + lax.erf(x * (1.0 / math.sqrt(2.0))))


def _rope128(y, cs, sg):
    lane = lax.broadcasted_iota(jnp.int32, y.shape, 1)
    swapped = jnp.where((lane & 63) < 32, pltpu.roll(y, 96, 1), pltpu.roll(y, 32, 1))
    return y * cs + swapped * sg


def _w_spec(k, tn, layer, shift=0):
    return pl.BlockSpec((None, k, tn), lambda i, j: (layer, 0, j + shift))


def _norm_rows(x_ref, g_ref, h_sc):
    @pl.when(pl.program_id(1) == 0)
    def _():
        h_sc[...] = (_rms(x_ref[...]) * g_ref[...]).astype(h_sc.dtype)


def _norm_mm_kernel(x_ref, g_ref, w_ref, o_ref, h_sc):
    _norm_rows(x_ref, g_ref, h_sc)
    o_ref[...] = jnp.dot(h_sc[...], w_ref[...], preferred_element_type=F32)


def norm_matmul(x, g, w, layer, name):
    m, k = x.shape
    n = w.shape[2]
    tm = _pick(m, (1024, 512, 256, 128))
    tn = 512
    return pl.pallas_call(
        _norm_mm_kernel,
        grid=(m // tm, n // tn),
        in_specs=[pl.BlockSpec((tm, k), lambda i, j: (i, 0)), pl.BlockSpec((1, k), lambda i, j: (0, 0)), _w_spec(k, tn, layer)],
        out_specs=pl.BlockSpec((tm, tn), lambda i, j: (i, j)),
        out_shape=jax.ShapeDtypeStruct((m, n), F32),
        scratch_shapes=[pltpu.VMEM((tm, k), BF16)],
        compiler_params=_params("parallel", "arbitrary"),
        name=name,
    )(x, g.reshape(1, k), w)


def _mm_res_kernel(a_ref, w_ref, x_ref, o_ref):
    o_ref[...] = x_ref[...] + jnp.dot(a_ref[...], w_ref[...], preferred_element_type=F32)


def matmul_residual(a, w, layer, residual, tm_cands, name):
    m, k = a.shape
    n = w.shape[2]
    tm = _pick(m, tm_cands)
    tn = 512
    return pl.pallas_call(
        _mm_res_kernel,
        grid=(m // tm, n // tn),
        in_specs=[pl.BlockSpec((tm, k), lambda i, j: (i, 0)), _w_spec(k, tn, layer), pl.BlockSpec((tm, tn), lambda i, j: (i, j))],
        out_specs=pl.BlockSpec((tm, tn), lambda i, j: (i, j)),
        out_shape=jax.ShapeDtypeStruct((m, n), F32),
        compiler_params=_params("parallel", "arbitrary"),
        name=name,
    )(a, w, residual)


def _swiglu_kernel(x_ref, g_ref, wg_ref, wu_ref, o_ref, h_sc):
    _norm_rows(x_ref, g_ref, h_sc)
    h = h_sc[...]
    gate = jnp.dot(h, wg_ref[...], preferred_element_type=F32)
    up = jnp.dot(h, wu_ref[...], preferred_element_type=F32)
    o_ref[...] = (_silu(gate) * up).astype(o_ref.dtype)


def swiglu_up(x, g, w_gu, layer):
    m, k = x.shape
    hid = w_gu.shape[2] // 2
    tm = _pick(m, (1024, 512, 256, 128))
    tn = 512
    nb = hid // tn
    return pl.pallas_call(
        _swiglu_kernel,
        grid=(m // tm, nb),
        in_specs=[
            pl.BlockSpec((tm, k), lambda i, j: (i, 0)), pl.BlockSpec((1, k), lambda i, j: (0, 0)),
            _w_spec(k, tn, layer), _w_spec(k, tn, layer, shift=nb),
        ],
        out_specs=pl.BlockSpec((tm, tn), lambda i, j: (i, j)),
        out_shape=jax.ShapeDtypeStruct((m, hid), BF16),
        scratch_shapes=[pltpu.VMEM((tm, k), BF16)],
        compiler_params=_params("parallel", "arbitrary"),
        name="swiglu_up",
    )(x, g.reshape(1, k), w_gu, w_gu)


def _merge_kernel(*refs, prompt_blocks):
    a_prompt, a_sample, w_refs, g_refs, o_ref = refs[0:3], refs[3:6], refs[6:9], refs[9:12], refs[12]
    i = pl.program_id(0)

    def run(a_refs):
        acc = None
        for a_ref, w_ref, g_ref in zip(a_refs, w_refs, g_refs):
            y = jnp.dot(a_ref[...].astype(BF16), w_ref[...], preferred_element_type=F32)
            y = jax.nn.sigmoid(g_ref[...]) * y
            acc = y if acc is None else acc + y
        o_ref[...] = acc.astype(o_ref.dtype)

    @pl.when(i < prompt_blocks)
    def _():
        run(a_prompt)

    @pl.when(i >= prompt_blocks)
    def _():
        run(a_sample)


def gated_merge(branches_prompt, branches_sample, weights, layer, z):
    mp, k = branches_prompt[0].shape
    ms = branches_sample[0].shape[0]
    n = weights[0].shape[2]
    tm = _pick(math.gcd(mp, ms), (1024, 512, 256, 128))
    tn = 256
    nb = n // tn
    npb = mp // tm
    p_spec = pl.BlockSpec((tm, k), lambda i, j: (jnp.minimum(i, npb - 1), 0))
    s_spec = pl.BlockSpec((tm, k), lambda i, j: (jnp.maximum(i - npb, 0), 0))
    w_spec = _w_spec(k, tn, layer)
    g_specs = [pl.BlockSpec((tm, tn), functools.partial(lambda i, j, b: (i, Z_GATE // tn + b * nb + j), b=b)) for b in range(N_BRANCH)]
    return pl.pallas_call(
        functools.partial(_merge_kernel, prompt_blocks=npb),
        grid=((mp + ms) // tm, nb),
        in_specs=[p_spec] * 3 + [s_spec] * 3 + [w_spec] * 3 + g_specs,
        out_specs=pl.BlockSpec((tm, tn), lambda i, j: (i, j)),
        out_shape=jax.ShapeDtypeStruct((mp + ms, n), BF16),
        compiler_params=_params("parallel", "arbitrary"),
        name="gated_merge",
    )(*branches_prompt, *branches_sample, *weights, z, z, z)


def _mla_prep_kernel(ql_ref, kv_ref, seg_ref, cs_ref, sg_ref, gq_ref, gkv_ref, gkr_ref, qo_ref, ckv_ref, kr_ref):
    qo_ref[...] = (_rms(ql_ref[...]) * gq_ref[...]).astype(qo_ref.dtype)
    ckv_ref[...] = _rms(kv_ref[...]) * gkv_ref[...]
    seg = seg_ref[...]
    lane = lax.broadcasted_iota(jnp.int32, seg.shape, 1)
    x = jnp.where(lane < MLA_ROPE, seg, 0.0)
    kr_ref[...] = _rope128(_rms(x, MLA_ROPE) * gkr_ref[...], cs_ref[...], sg_ref[...])


def mla_prep(z, cs, sg, g_q, g_kv, g_kr_pad):
    m = z.shape[0]
    tm = _pick(m, (512, 256, 128))
    row = lambda w, c: pl.BlockSpec((tm, w), lambda i: (i, c))
    par = lambda w: pl.BlockSpec((1, w), lambda i: (0, 0))
    return pl.pallas_call(
        _mla_prep_kernel,
        grid=(m // tm,),
        in_specs=[
            row(MLA_Q_RANK, Z_QLAT // MLA_Q_RANK), row(MLA_KV_RANK, Z_KV // MLA_KV_RANK), row(LANES, Z_SEG // LANES),
            row(LANES, 0), row(LANES, 0), par(MLA_Q_RANK), par(MLA_KV_RANK), par(LANES),
        ],
        out_specs=[row(MLA_Q_RANK, 0), row(MLA_KV_RANK, 0), row(LANES, 0)],
        out_shape=[
            jax.ShapeDtypeStruct((m, MLA_Q_RANK), BF16),
            jax.ShapeDtypeStruct((m, MLA_KV_RANK), F32),
            jax.ShapeDtypeStruct((m, LANES), F32),
        ],
        compiler_params=_params("parallel"),
        name="mla_prep",
    )(z, z, z, cs, sg, g_q.reshape(1, -1), g_kv.reshape(1, -1), g_kr_pad)


def _q_kernel(a_ref, w_ref, cs_ref, sg_ref, gn_ref, gr_ref, o_ref, *, scale):
    a = a_ref[...]
    cs, sg = cs_ref[...], sg_ref[...]
    hw = MLA_NOPE + LANES
    q = jnp.dot(a, w_ref[...], preferred_element_type=F32)
    for h in range(MLA_HEADS):
        qn = q[:, h * hw:h * hw + MLA_NOPE]
        o_ref[:, h * hw:h * hw + MLA_NOPE] = (_rms(qn) * gn_ref[...] * scale).astype(o_ref.dtype)
        qr = q[:, h * hw + MLA_NOPE:(h + 1) * hw]
        qr = _rope128(_rms(qr, MLA_ROPE) * gr_ref[...], cs, sg)
        o_ref[:, h * hw + MLA_NOPE:(h + 1) * hw] = (qr * scale).astype(o_ref.dtype)


def q_project(q_lat, w_uq_pad, cs, sg, g_qn, g_qr_pad):
    m, k = q_lat.shape
    n = w_uq_pad.shape[1]
    tm = _pick(m, (512, 256, 128))
    scale = (MLA_NOPE + MLA_ROPE) ** -0.5
    return pl.pallas_call(
        functools.partial(_q_kernel, scale=scale),
        grid=(m // tm,),
        in_specs=[
            pl.BlockSpec((tm, k), lambda i: (i, 0)), pl.BlockSpec((k, n), lambda i: (0, 0)),
            pl.BlockSpec((tm, LANES), lambda i: (i, 0)), pl.BlockSpec((tm, LANES), lambda i: (i, 0)),
            pl.BlockSpec((1, LANES), lambda i: (0, 0)), pl.BlockSpec((1, LANES), lambda i: (0, 0)),
        ],
        out_specs=pl.BlockSpec((tm, n), lambda i: (i, 0)),
        out_shape=jax.ShapeDtypeStruct((m, n), BF16),
        compiler_params=_params("parallel"),
        name="q_project",
    )(q_lat, w_uq_pad, cs, sg, g_qn.reshape(1, -1), g_qr_pad)


def _kv_kernel(a_ref, wk_ref, wv_ref, kr_ref, gk_ref, ko_ref, vo_ref):
    a = a_ref[...].astype(BF16)
    krb = kr_ref[...].astype(BF16)
    hw = MLA_NOPE + LANES
    kn_all = jnp.dot(a, wk_ref[...], preferred_element_type=F32)
    for h in range(MLA_HEADS):
        kn = kn_all[:, h * MLA_NOPE:(h + 1) * MLA_NOPE]
        ko_ref[:, h * hw:h * hw + MLA_NOPE] = (_rms(kn) * gk_ref[...]).astype(ko_ref.dtype)
        ko_ref[:, h * hw + MLA_NOPE:(h + 1) * hw] = krb
    vo_ref[...] = jnp.dot(a, wv_ref[...], preferred_element_type=F32).astype(vo_ref.dtype)


def kv_expand(ckv, kr_pad, w_uk, w_uv, g_kn, rows):
    tm = _pick(rows, (512, 256, 128))
    hw = MLA_NOPE + LANES
    return pl.pallas_call(
        _kv_kernel,
        grid=(rows // tm,),
        in_specs=[
            pl.BlockSpec((tm, MLA_KV_RANK), lambda i: (i, 0)),
            pl.BlockSpec(w_uk.shape, lambda i: (0, 0)), pl.BlockSpec(w_uv.shape, lambda i: (0, 0)),
            pl.BlockSpec((tm, LANES), lambda i: (i, 0)), pl.BlockSpec((1, MLA_NOPE), lambda i: (0, 0)),
        ],
        out_specs=[pl.BlockSpec((tm, MLA_HEADS * hw), lambda i: (i, 0)), pl.BlockSpec((tm, MLA_HEADS * MLA_V), lambda i: (i, 0))],
        out_shape=[jax.ShapeDtypeStruct((rows, MLA_HEADS * hw), BF16), jax.ShapeDtypeStruct((rows, MLA_HEADS * MLA_V), BF16)],
        compiler_params=_params("parallel"),
        name="kv_expand",
    )(ckv, w_uk, w_uv, kr_pad, g_kn.reshape(1, -1))


def _flash_kernel(q_ref, k_ref, v_ref, o_ref, *, tq, nq, heads):
    qi = pl.program_id(2)
    hw = MLA_NOPE + LANES
    for i in range(nq):
        @pl.when(qi == i)
        def _(i=i):
            kend = (i + 1) * tq
            s = [lax.dot_general(q_ref[:, h * hw:(h + 1) * hw], k_ref[0:kend, h * hw:(h + 1) * hw], NT, preferred_element_type=F32)
                 for h in range(heads)]
            row = i * tq + lax.broadcasted_iota(jnp.int32, s[0].shape, 0)
            col = lax.broadcasted_iota(jnp.int32, s[0].shape, 1)
            p, l = [], []
            for h in range(heads):
                sh = jnp.where(col <= row, s[h], -jnp.inf)
                ph = jnp.exp(sh - jnp.max(sh, axis=-1, keepdims=True))
                l.append(jnp.sum(ph, axis=-1, keepdims=True))
                p.append(ph.astype(BF16))
            o = [jnp.dot(p[h], v_ref[0:kend, h * MLA_V:(h + 1) * MLA_V], preferred_element_type=F32) for h in range(heads)]
            for h in range(heads):
                o_ref[:, h * MLA_V:(h + 1) * MLA_V] = (o[h] / l[h]).astype(o_ref.dtype)


def mla_prompt_attention(qcat, kcat, v, batch, seq):
    tq = _pick(seq, (256, 128))
    nq = seq // tq
    heads = 4
    hw = heads * (MLA_NOPE + LANES)
    vw = heads * MLA_V
    return pl.pallas_call(
        functools.partial(_flash_kernel, tq=tq, nq=nq, heads=heads),
        grid=(batch, MLA_HEADS // heads, nq),
        in_specs=[
            pl.BlockSpec((tq, hw), lambda b, h, i: (b * nq + i, h)),
            pl.BlockSpec((seq, hw), lambda b, h, i: (b, h)),
            pl.BlockSpec((seq, vw), lambda b, h, i: (b, h)),
        ],
        out_specs=pl.BlockSpec((tq, vw), lambda b, h, i: (b * nq + i, h)),
        out_shape=jax.ShapeDtypeStruct((batch * seq, MLA_HEADS * MLA_V), BF16),
        compiler_params=_params("parallel", "parallel", "arbitrary"),
        name="mla_prompt_attention",
    )(qcat, kcat, v)


def _absorb_kernel(q_ref, w_ref, g_ref, o_ref):
    q = q_ref[:, 0:MLA_NOPE].astype(F32) * g_ref[...]
    o_ref[0] = jnp.dot(q.astype(BF16), w_ref[0], preferred_element_type=F32)


def absorb_queries(qcat, w_ukt, g_kn, row0, rows):
    hw = MLA_NOPE + LANES
    return pl.pallas_call(
        _absorb_kernel,
        grid=(MLA_HEADS,),
        in_specs=[
            pl.BlockSpec((rows, hw), lambda h: (row0 // rows, h)),
            pl.BlockSpec((1, MLA_NOPE, MLA_KV_RANK), lambda h: (h, 0, 0)),
            pl.BlockSpec((1, MLA_NOPE), lambda h: (0, 0)),
        ],
        out_specs=pl.BlockSpec((1, rows, MLA_KV_RANK), lambda h: (h, 0, 0)),
        out_shape=jax.ShapeDtypeStruct((MLA_HEADS, rows, MLA_KV_RANK), F32),
        compiler_params=_params("parallel"),
        name="absorb_queries",
    )(qcat, w_ukt, g_kn.reshape(1, -1))


def _unabsorb_kernel(a_ref, w_ref, o_ref):
    o_ref[...] = jnp.dot(a_ref[0].astype(BF16), w_ref[0], preferred_element_type=F32).astype(o_ref.dtype)


def unabsorb_values(o_lat, w_uv_h):
    _, rows, r = o_lat.shape
    return pl.pallas_call(
        _unabsorb_kernel,
        grid=(MLA_HEADS,),
        in_specs=[pl.BlockSpec((1, rows, r), lambda h: (h, 0, 0)), pl.BlockSpec((1, r, MLA_V), lambda h: (h, 0, 0))],
        out_specs=pl.BlockSpec((rows, MLA_V), lambda h: (0, h)),
        out_shape=jax.ShapeDtypeStruct((rows, MLA_HEADS * MLA_V), BF16),
        compiler_params=_params("parallel"),
        name="unabsorb_values",
    )(o_lat, w_uv_h)


def _sattn_kernel(pt_ref, qa_ref, qr_ref, cn_ref, kn_ref, wk_ref, ckv_hbm, krt_hbm, o_ref,
                  ckv_buf, krt_buf, ckb, s_sc, sem, *, layer, tile_pages, n_pages, page, dec):
    b = pl.program_id(0)
    nb = pl.num_programs(0)
    slot = b % 2
    ntiles = n_pages // tile_pages
    tk = tile_pages * page

    def tile_copies(bb, j, sl):
        out = []
        for g in range(tile_pages):
            slot_page = j * tile_pages + g
            pg = pt_ref[bb * n_pages + slot_page]
            out.append(pltpu.make_async_copy(ckv_hbm.at[layer, pg], ckv_buf.at[sl, slot_page], sem.at[sl, 0]))
            out.append(pltpu.make_async_copy(krt_hbm.at[layer, pg], krt_buf.at[sl, slot_page], sem.at[sl, 1]))
        return out

    @pl.when(b == 0)
    def _():
        def first(j, carry):
            for cp in tile_copies(b, j, slot):
                cp.start()
            return carry
        lax.fori_loop(0, ntiles, first, 0)

    pltpu.make_async_copy(ckv_hbm.at[layer, pl.ds(0, n_pages)], ckv_buf.at[slot], sem.at[slot, 0]).wait()
    pltpu.make_async_copy(krt_hbm.at[layer, pl.ds(0, n_pages)], krt_buf.at[slot], sem.at[slot, 1]).wait()

    qa = qa_ref[0].astype(BF16)
    qr = qr_ref[0].astype(BF16)

    def nope_scores(ck):
        kn = lax.dot_general(wk_ref[...], ck, NT, preferred_element_type=F32)
        raw = lax.dot_general(qa, ck, NT, preferred_element_type=F32)
        inv = []
        for h in range(MLA_HEADS):
            kh = kn[h * MLA_NOPE:(h + 1) * MLA_NOPE]
            r = lax.rsqrt(jnp.sum(kh * kh, axis=0, keepdims=True) * (1.0 / MLA_NOPE) + EPS)
            inv.append(jnp.broadcast_to(r, (dec, r.shape[1])))
        return raw * jnp.concatenate(inv, axis=0)

    def tile(j, carry):
        @pl.when(b + 1 < nb)
        def _():
            for cp in tile_copies(b + 1, j, 1 - slot):
                cp.start()

        r0 = pl.multiple_of(j * tk, tk)
        p0 = j * tile_pages
        ck = ckv_buf[slot, pl.ds(p0, tile_pages)].reshape(tk, ckv_buf.shape[-1]).astype(BF16)
        ckb[pl.ds(r0, tk), :] = ck
        krt = jnp.concatenate([krt_buf[slot, p0 + g] for g in range(tile_pages)], axis=1).astype(BF16)
        rope = jnp.dot(qr, krt, preferred_element_type=F32)
        s_sc[j] = nope_scores(ck) + rope
        return carry
    lax.fori_loop(0, ntiles, tile, 0)

    pad = page - dec
    cnew = jnp.concatenate([cn_ref[...], jnp.zeros((pad, cn_ref.shape[1]), F32)], axis=0).astype(BF16)
    knew = jnp.concatenate([kn_ref[:, 0:MLA_ROPE], jnp.zeros((pad, MLA_ROPE), F32)], axis=0).astype(BF16)
    s_new = nope_scores(cnew) + lax.dot_general(qr, knew, NT, preferred_element_type=F32)
    qpos = lax.broadcasted_iota(jnp.int32, s_new.shape, 0) % dec
    kpos = lax.broadcasted_iota(jnp.int32, s_new.shape, 1)
    s_new = jnp.where(kpos <= qpos, s_new, -jnp.inf)

    s_max = s_sc[0]
    for j in range(1, ntiles):
        s_max = jnp.maximum(s_max, s_sc[j])
    m = jnp.maximum(jnp.max(s_new, axis=-1, keepdims=True), jnp.max(s_max, axis=-1, keepdims=True))
    p_new = jnp.exp(s_new - m)
    p = [jnp.exp(s_sc[j] - m) for j in range(ntiles)]
    p_sum = p[0]
    for pj in p[1:]:
        p_sum = p_sum + pj
    l = jnp.sum(p_new, axis=-1, keepdims=True) + jnp.sum(p_sum, axis=-1, keepdims=True)
    p_all = jnp.concatenate([pj.astype(BF16) for pj in p], axis=1)
    acc = jnp.dot(p_all, ckb[...], preferred_element_type=F32) + jnp.dot(p_new.astype(BF16), cnew, preferred_element_type=F32)
    o_ref[0] = acc / l


def mla_sample_attention(q_abs, q_rope, ckv, kr_pad, w_ukt2, cache_ckv, cache_krope_t, page_table, layer, row0, dec):
    nb, nq, r = q_abs.shape
    n_pages = page_table.shape[1]
    page = cache_ckv.shape[2]
    tile_pages = _pick(n_pages, (8, 4, 2, 1))
    ntiles = n_pages // tile_pages
    tk = tile_pages * page
    grid_spec = pltpu.PrefetchScalarGridSpec(
        num_scalar_prefetch=1,
        grid=(nb,),
        in_specs=[
            pl.BlockSpec((1, nq, r), lambda b, pt: (b, 0, 0)),
            pl.BlockSpec((1, nq, MLA_ROPE), lambda b, pt: (b, 0, 0)),
            pl.BlockSpec((dec, r), lambda b, pt: (row0 // dec + b, 0)),
            pl.BlockSpec((dec, LANES), lambda b, pt: (row0 // dec + b, 0)),
            pl.BlockSpec(w_ukt2.shape, lambda b, pt: (0, 0)),
            pl.BlockSpec(memory_space=pl.ANY),
            pl.BlockSpec(memory_space=pl.ANY),
        ],
        out_specs=pl.BlockSpec((1, nq, r), lambda b, pt: (b, 0, 0)),
        scratch_shapes=[
            pltpu.VMEM((2, n_pages, page, r), F32),
            pltpu.VMEM((2, n_pages, MLA_ROPE, page), F32),
            pltpu.VMEM((n_pages * page, r), BF16),
            pltpu.VMEM((ntiles, nq, tk), F32),
            pltpu.SemaphoreType.DMA((2, 2)),
        ],
    )
    return pl.pallas_call(
        functools.partial(_sattn_kernel, layer=layer, tile_pages=tile_pages, n_pages=n_pages, page=page, dec=dec),
        grid_spec=grid_spec,
        out_shape=jax.ShapeDtypeStruct((nb, nq, r), F32),
        compiler_params=_params("arbitrary"),
        name="mla_sample_attention",
    )(page_table.reshape(-1), q_abs, q_rope, ckv, kr_pad, w_ukt2, cache_ckv, cache_krope_t)


def _chunk_masks(c):
    ii = lax.broadcasted_iota(jnp.int32, (c, c), 0)
    jj = lax.broadcasted_iota(jnp.int32, (c, c), 1)
    return jj <= ii, jj < ii


def _split_bf16(x, terms):
    parts = []
    for _ in range(terms):
        p = x.astype(BF16)
        parts.append(p)
        x = x - p.astype(F32)
    return parts


def _dot_split(a, b):
    ah, al = _split_bf16(a, 2)
    bh, bl = _split_bf16(b, 2)
    dot = functools.partial(jnp.dot, preferred_element_type=F32)
    return dot(ah, bh) + (dot(ah, bl) + dot(al, bh))


def _gdn_prep_kernel(x_ref, seg_ref, c0_ref, cw_ref, al_ref, dt_ref, qkv_ref, gb_ref, cn_ref, a_ref, ext, *, nsq, rows, chunk):
    t = pl.program_id(1)
    hk = GDN_HEADS * GDN_DK
    tail = GDN_CONV - 1
    tb = nsq * rows
    cw = cw_ref[...]
    for sq in range(nsq):
        @pl.when(t == 0)
        def _(sq=sq):
            ext[sq, 8 - tail:8, :] = c0_ref[sq]

        ext[sq, 8:8 + rows, :] = x_ref[sq * rows:(sq + 1) * rows, :]
        y = cw[0:1] * ext[sq, 5:5 + rows, :]
        for j in range(1, GDN_CONV):
            y = y + cw[j:j + 1] * ext[sq, 5 + j:5 + j + rows, :]
        new_tail = ext[sq, 8 + rows - tail:8 + rows, :]
        cn_ref[sq] = new_tail
        ext[sq, 8 - tail:8, :] = new_tail
        qkv_ref[sq * rows:(sq + 1) * rows, :] = _silu(y)

    seg = seg_ref[...]
    lane = lax.broadcasted_iota(jnp.int32, seg.shape, 1)
    beta = jax.nn.sigmoid(seg)
    g = -jnp.exp(al_ref[...]) * jax.nn.softplus(seg + dt_ref[...])
    ri = lax.broadcasted_iota(jnp.int32, (tb, tb), 0)
    ci = lax.broadcasted_iota(jnp.int32, (tb, tb), 1)
    cum = jnp.where((ri // chunk == ci // chunk) & (ci <= ri), 1.0, 0.0).astype(BF16)
    gcum = sum(jnp.dot(cum, p, preferred_element_type=F32) for p in reversed(_split_bf16(g, 3)))
    gb = jnp.where((lane >= SEG_A) & (lane < SEG_B), gcum, beta)
    gb_ref[...] = gb
    gcum_t = gcum.T

    _, strict = _chunk_masks(chunk)
    for h in range(GDN_HEADS):
        qh = qkv_ref[:, h * GDN_DK:(h + 1) * GDN_DK]
        qh = qh * lax.rsqrt(jnp.sum(qh * qh, axis=-1, keepdims=True) + EPS) * (GDN_DK ** -0.5)
        kh = qkv_ref[:, hk + h * GDN_DK:hk + (h + 1) * GDN_DK]
        kh = kh * lax.rsqrt(jnp.sum(kh * kh, axis=-1, keepdims=True) + EPS)
        qkv_ref[:, h * GDN_DK:(h + 1) * GDN_DK] = qh
        qkv_ref[:, hk + h * GDN_DK:hk + (h + 1) * GDN_DK] = kh
        kb = (kh * beta[:, SEG_B + h:SEG_B + h + 1]).astype(BF16)
        kh = kh.astype(BF16)
        for c in range(tb // chunk):
            r0, r1 = c * chunk, (c + 1) * chunk
            gc = gcum[r0:r1, SEG_A + h:SEG_A + h + 1]
            gr = gcum_t[SEG_A + h:SEG_A + h + 1, r0:r1]
            decay = jnp.exp(jnp.where(strict, gc - gr, 0.0))
            kk = lax.dot_general(kb[r0:r1], kh[r0:r1], NT, preferred_element_type=F32)
            a_ref[c, h] = jnp.where(strict, kk * decay, 0.0)


def gdn_prep(z, conv0, layer, conv_w, a_log_row, dt_row, row0, nseq, seqlen, nsq, rows, chunk):
    nt = seqlen // rows
    tb = nsq * rows
    nc = tb // chunk
    base = row0 // tb
    assert nsq == 1 or nt == 1
    return pl.pallas_call(
        functools.partial(_gdn_prep_kernel, nsq=nsq, rows=rows, chunk=chunk),
        grid=(nseq // nsq, nt),
        in_specs=[
            pl.BlockSpec((tb, GDN_QKV), lambda b, t: (base + b * nt + t, Z_QKV // GDN_QKV)),
            pl.BlockSpec((tb, LANES), lambda b, t: (base + b * nt + t, Z_SEG // LANES)),
            pl.BlockSpec((None, nsq, GDN_CONV - 1, GDN_QKV), lambda b, t: (layer, b, 0, 0)),
            pl.BlockSpec((GDN_CONV, GDN_QKV), lambda b, t: (0, 0)),
            pl.BlockSpec((1, LANES), lambda b, t: (0, 0)),
            pl.BlockSpec((1, LANES), lambda b, t: (0, 0)),
        ],
        out_specs=[
            pl.BlockSpec((tb, GDN_QKV), lambda b, t: (b * nt + t, 0)),
            pl.BlockSpec((tb, LANES), lambda b, t: (b * nt + t, 0)),
            pl.BlockSpec((nsq, GDN_CONV - 1, GDN_QKV), lambda b, t: (b, 0, 0)),
            pl.BlockSpec((nc, GDN_HEADS, chunk, chunk), lambda b, t: (b * nt + t, 0, 0, 0)),
        ],
        out_shape=[
            jax.ShapeDtypeStruct((nseq * seqlen, GDN_QKV), F32),
            jax.ShapeDtypeStruct((nseq * seqlen, LANES), F32),
            jax.ShapeDtypeStruct((nseq, GDN_CONV - 1, GDN_QKV), F32),
            jax.ShapeDtypeStruct((nseq * seqlen // chunk, GDN_HEADS, chunk, chunk), F32),
        ],
        scratch_shapes=[pltpu.VMEM((nsq, rows + 8, GDN_QKV), F32)],
        compiler_params=_params("parallel", "arbitrary"),
        name="gdn_prep",
    )(z, z, conv0, conv_w, a_log_row, dt_row)


def _trinv_kernel(a_ref, o_ref, t_sc, *, n, kb):
    i = pl.program_id(0)
    for b in range(n // kb):
        k0 = b * kb

        def body(j, acc, k0=k0):
            return acc + a_ref[0, j][None] * t_sc[j, k0:k0 + kb]

        acc = lax.fori_loop(k0, i, body, jnp.zeros((kb,) + t_sc.shape[2:], F32))
        kidx = k0 + lax.broadcasted_iota(jnp.int32, acc.shape, 0)
        row = jnp.where(kidx == i, 1.0, 0.0) - acc
        t_sc[i, k0:k0 + kb] = row
        o_ref[0, k0:k0 + kb] = row


def unit_lower_inverse(a):
    nsys, n, _ = a.shape
    tile = 8 * LANES
    npad = -(-nsys // tile) * tile
    at = jnp.transpose(a, (1, 2, 0))
    if npad != nsys:
        at = jnp.pad(at, ((0, 0), (0, 0), (0, npad - nsys)))
    at = at.reshape(n, n, npad // LANES, LANES)
    kb = min(n, 16)
    outs = []
    for s in range(npad // tile):
        blk = at[:, :, s * 8:(s + 1) * 8]
        outs.append(pl.pallas_call(
            functools.partial(_trinv_kernel, n=n, kb=kb),
            grid=(n,),
            in_specs=[pl.BlockSpec((1, n, 8, LANES), lambda i: (i, 0, 0, 0))],
            out_specs=pl.BlockSpec((1, n, 8, LANES), lambda i: (i, 0, 0, 0)),
            out_shape=jax.ShapeDtypeStruct((n, n, 8, LANES), F32),
            scratch_shapes=[pltpu.VMEM((n, n, 8, LANES), F32)],
            compiler_params=_params("arbitrary"),
            name="unit_lower_inverse",
        )(blk))
    t = outs[0] if len(outs) == 1 else jnp.concatenate(outs, axis=2)
    t = t.reshape(n, n, npad)[:, :, :nsys]
    return jnp.transpose(t, (2, 0, 1))


def _gdn_scan_kernel(qkv_ref, gb_ref, z_ref, ti_ref, s0_ref, gon_ref, o_ref, so_ref, s_sc, *, nsq, rows, chunk):
    t = pl.program_id(1)
    hk = GDN_HEADS * GDN_DK

    @pl.when(t == 0)
    def _():
        s_sc[...] = s0_ref[...]

    gb = gb_ref[...]
    gb_t = gb.T
    incl, _ = _chunk_masks(chunk)
    ncs = rows // chunk
    pairs = [(sq, h) for sq in range(nsq) for h in range(GDN_HEADS)]
    dot = functools.partial(jnp.dot, preferred_element_type=F32)
    for c in range(ncs):
        span = {(sq, h): (sq * rows + c * chunk, sq * rows + (c + 1) * chunk) for sq, h in pairs}
        st1, st2, st3, st4 = {}, {}, {}, {}
        for sq, h in pairs:
            r0, r1 = span[sq, h]
            gc = gb[r0:r1, SEG_A + h:SEG_A + h + 1]
            gr = gb_t[SEG_A + h:SEG_A + h + 1, r0:r1]
            beta = gb[r0:r1, SEG_B + h:SEG_B + h + 1]
            q = qkv_ref[r0:r1, h * GDN_DK:(h + 1) * GDN_DK]
            k = qkv_ref[r0:r1, hk + h * GDN_DK:hk + (h + 1) * GDN_DK]
            v = qkv_ref[r0:r1, 2 * hk + h * GDN_DV:2 * hk + (h + 1) * GDN_DV]
            egc = jnp.exp(gc)
            glast = gc[chunk - 1:chunk, :]
            th, tl = _split_bf16(ti_ref[sq * ncs + c, h], 2)
            rhs = jnp.concatenate([v * beta, k * (beta * egc)], axis=1)
            rh, rl = _split_bf16(rhs, 2)
            sol = (dot(th, rh), dot(th, rl), dot(tl, rh))
            qk = lax.dot_general(q.astype(BF16), k.astype(BF16), NT, preferred_element_type=F32)
            decay = jnp.exp(jnp.where(incl, gc - gr, 0.0))
            k_tail = (k * jnp.exp(glast - gc)).astype(BF16)
            st1[sq, h] = (sol, qk, decay, q * egc, k_tail, jnp.exp(glast))
        for sq, h in pairs:
            sol, qk, decay, qe, k_tail, dec_last = st1[sq, h]
            uw = sol[0] + (sol[1] + sol[2])
            wq = jnp.concatenate([uw[:, GDN_DV:], qe], axis=0).astype(BF16)
            attn = jnp.where(incl, qk * decay, 0.0).astype(BF16)
            st2[sq, h] = (uw[:, :GDN_DV], wq, attn, k_tail, dec_last)
        for sq, h in pairs:
            u, wq, attn, k_tail, dec_last = st2[sq, h]
            s_h = s_sc[sq, h]
            st3[sq, h] = (u, dot(wq, s_h.astype(BF16)), attn, k_tail, s_h * dec_last)
        for sq, h in pairs:
            u, ws, attn, k_tail, s_dec = st3[sq, h]
            v_new_b = (u - ws[:chunk]).astype(BF16)
            st4[sq, h] = ws[chunk:] + dot(attn, v_new_b)
            s_sc[sq, h] = s_dec + lax.dot_general(k_tail, v_new_b, TN, preferred_element_type=F32)
        for sq, h in pairs:
            r0, r1 = span[sq, h]
            zg = z_ref[r0:r1, h * GDN_DV:(h + 1) * GDN_DV]
            o_ref[r0:r1, h * GDN_DV:(h + 1) * GDN_DV] = (_rms(st4[sq, h]) * gon_ref[...] * _silu(zg)).astype(o_ref.dtype)

    @pl.when(t == pl.num_programs(1) - 1)
    def _():
        so_ref[...] = s_sc[...]


def gdn_scan(qkv, gb, z, tinv, s0, layer, g_onorm, row0, nseq, seqlen, nsq, rows, chunk, out_dtype):
    nt = seqlen // rows
    tb = nsq * rows
    nc = tb // chunk
    base = row0 // tb
    state = (nsq, GDN_HEADS, GDN_DK, GDN_DV)
    return pl.pallas_call(
        functools.partial(_gdn_scan_kernel, nsq=nsq, rows=rows, chunk=chunk),
        grid=(nseq // nsq, nt),
        in_specs=[
            pl.BlockSpec((tb, GDN_QKV), lambda b, t: (b * nt + t, 0)),
            pl.BlockSpec((tb, LANES), lambda b, t: (b * nt + t, 0)),
            pl.BlockSpec((tb, GDN_W), lambda b, t: (base + b * nt + t, Z_Z // GDN_W)),
            pl.BlockSpec((nc, GDN_HEADS, chunk, chunk), lambda b, t: (b * nt + t, 0, 0, 0)),
            pl.BlockSpec((None,) + state, lambda b, t: (layer, b, 0, 0, 0)),
            pl.BlockSpec((1, GDN_DV), lambda b, t: (0, 0)),
        ],
        out_specs=[
            pl.BlockSpec((tb, GDN_W), lambda b, t: (b * nt + t, 0)),
            pl.BlockSpec(state, lambda b, t: (b, 0, 0, 0)),
        ],
        out_shape=[
            jax.ShapeDtypeStruct((nseq * seqlen, GDN_W), out_dtype),
            jax.ShapeDtypeStruct((nseq, GDN_HEADS, GDN_DK, GDN_DV), F32),
        ],
        scratch_shapes=[pltpu.VMEM(state, F32)],
        compiler_params=_params("parallel", "arbitrary"),
        name="gdn_scan",
    )(qkv, gb, z, tinv, s0, g_onorm.reshape(1, -1))


def gdn_branch(z, conv0, s0, layer, conv_w, a_log_row, dt_row, g_onorm, row0, nseq, seqlen):
    chunk = min(GDN_CHUNK, seqlen)
    rows = _pick(seqlen, (256, 128, 64, 8))
    nsq = _pick(nseq, (8, 4, 2, 1)) if rows == seqlen and rows < GDN_CHUNK else 1
    out_dtype = BF16 if chunk % 16 == 0 else F32
    qkv, gb, conv_new, a = gdn_prep(z, conv0, layer, conv_w, a_log_row, dt_row, row0, nseq, seqlen, nsq, rows, chunk)
    tinv = unit_lower_inverse(a.reshape(-1, chunk, chunk)).reshape(a.shape)
    o, s_new = gdn_scan(qkv, gb, z, tinv, s0, layer, g_onorm, row0, nseq, seqlen, nsq, rows, chunk, out_dtype)
    return o, conv_new, s_new


def _layer_norm(v, g, b):
    mu = jnp.mean(v, axis=-1, keepdims=True)
    d = v - mu
    return d * lax.rsqrt(jnp.mean(d * d, axis=-1, keepdims=True) + EPS) * g + b


def _gmlp_prompt_kernel(u_ref, v_ref, lg_ref, lb_ref, w_ref, bias_ref, o_ref, *, rows):
    vn = _layer_norm(_gelu(v_ref[...]), lg_ref[...], lb_ref[...]).astype(BF16)
    u = _gelu(u_ref[...])
    tril, _ = _chunk_masks(GMLP_CHUNK)
    bias = bias_ref[...]
    blocks = [(g, c) for g in range(GMLP_GROUPS) for c in range(rows // GMLP_CHUNK)]
    wg = [jnp.where(tril, w_ref[g], 0.0).astype(BF16) for g in range(GMLP_GROUPS)]
    s = {}
    for g, c in blocks:
        s[g, c] = jnp.dot(wg[g], vn[c * GMLP_CHUNK:(c + 1) * GMLP_CHUNK, g * GMLP_GROUP_W:(g + 1) * GMLP_GROUP_W], preferred_element_type=F32)
    for g, c in blocks:
        r0, r1 = c * GMLP_CHUNK, (c + 1) * GMLP_CHUNK
        c0, c1 = g * GMLP_GROUP_W, (g + 1) * GMLP_GROUP_W
        o_ref[r0:r1, c0:c1] = (u[r0:r1, c0:c1] * (s[g, c] + bias[:, c0:c1])).astype(o_ref.dtype)


def gmlp_prompt(z, ln_g, ln_b, w_s, bias_full, nrows):
    rows = _pick(nrows, (512, 256, 128))
    row = lambda cb: pl.BlockSpec((rows, GMLP_W), lambda i: (i, cb))
    par = lambda: pl.BlockSpec((1, GMLP_W), lambda i: (0, 0))
    return pl.pallas_call(
        functools.partial(_gmlp_prompt_kernel, rows=rows),
        grid=(nrows // rows,),
        in_specs=[
            row(Z_U // GMLP_W), row(Z_V // GMLP_W), par(), par(),
            pl.BlockSpec(w_s.shape, lambda i: (0, 0, 0)),
            pl.BlockSpec(bias_full.shape, lambda i: (0, 0)),
        ],
        out_specs=row(0),
        out_shape=jax.ShapeDtypeStruct((nrows, GMLP_W), BF16),
        compiler_params=_params("parallel"),
        name="gmlp_prompt",
    )(z, z, ln_g.reshape(1, -1), ln_b.reshape(1, -1), w_s, bias_full)


def _gmlp_sample_kernel(u_ref, v_ref, lg_ref, lb_ref, wt_ref, b_ref, o_ref, vr_ref, *, rows, dec):
    vn = _layer_norm(_gelu(v_ref[...]), lg_ref[...], lb_ref[...])
    vr_ref[...] = vn
    v3 = vn.reshape(rows // dec, dec, GMLP_W)
    tpos = lax.broadcasted_iota(jnp.int32, (dec, GMLP_W), 0)
    s3 = jnp.broadcast_to(b_ref[...][None], v3.shape)
    for j in range(dec):
        wj = jnp.where(tpos >= j, wt_ref[j], 0.0)
        s3 = s3 + wj[None] * v3[:, j:j + 1, :]
    u3 = _gelu(u_ref[...]).reshape(rows // dec, dec, GMLP_W)
    o_ref[...] = (u3 * s3).reshape(rows, GMLP_W).astype(o_ref.dtype)


def gmlp_sample(z, ln_g, ln_b, wt, b_dec, row0, nrows, dec):
    rows = _pick(nrows, (256, 128))
    base = row0 // rows
    row = lambda cb, off: pl.BlockSpec((rows, GMLP_W), lambda i: (off + i, cb))
    par = lambda: pl.BlockSpec((1, GMLP_W), lambda i: (0, 0))
    return pl.pallas_call(
        functools.partial(_gmlp_sample_kernel, rows=rows, dec=dec),
        grid=(nrows // rows,),
        in_specs=[
            row(Z_U // GMLP_W, base), row(Z_V // GMLP_W, base), par(), par(),
            pl.BlockSpec(wt.shape, lambda i: (0, 0, 0)),
            pl.BlockSpec(b_dec.shape, lambda i: (0, 0)),
        ],
        out_specs=[row(0, 0), row(0, 0)],
        out_shape=[jax.ShapeDtypeStruct((nrows, GMLP_W), BF16), jax.ShapeDtypeStruct((nrows, GMLP_W), F32)],
        compiler_params=_params("parallel"),
        name="gmlp_sample",
    )(z, z, ln_g.reshape(1, -1), ln_b.reshape(1, -1), wt, b_dec)


def _rope_tables(pos):
    inv = ROPE_THETA ** (-jnp.arange(0, MLA_ROPE, 2, dtype=F32) / MLA_ROPE)
    ang = pos.astype(F32)[:, None] * inv[None, :]
    cos, sin = jnp.cos(ang), jnp.sin(ang)
    zero = jnp.zeros((pos.shape[0], LANES - MLA_ROPE), F32)
    return jnp.concatenate([cos, cos, zero], axis=1), jnp.concatenate([-sin, sin, zero], axis=1)


def _pad_lanes(v, offset=0):
    return jnp.zeros((1, LANES), F32).at[0, offset:offset + v.shape[0]].set(v)


def kernel(x_prompt, x_sample, cache_ckv, cache_krope, state_gdn, state_conv, page_table, norm_mix_g, w_in, mla_q_norm_g, mla_w_uq, mla_qn_g, mla_qr_g, mla_kv_norm_g, mla_kr_g, mla_w_uk, mla_kn_g, mla_w_uv, gdn_conv_w, gdn_a_log, gdn_dt_bias, gdn_o_norm_g, gmlp_ln_g, gmlp_ln_b, gmlp_w_s, gmlp_b_s, w_br_mla, w_br_gdn, w_br_gmlp, w_o, norm_ffn_g, ffn_w_gu, ffn_w_down):
    bp, seq, d = x_prompt.shape
    bs, dec, _ = x_sample.shape
    depth = w_in.shape[0]
    mp, ms = bp * seq, bs * dec
    n_past = page_table.shape[1] * cache_ckv.shape[2]
    assert d == D_MODEL and dec == 8 and mp % ms == 0 and seq % GMLP_CHUNK == 0 and seq % GDN_CHUNK == 0

    x = jnp.concatenate([x_prompt.reshape(mp, d), x_sample.reshape(ms, d)], axis=0)
    pos = jnp.concatenate([jnp.tile(jnp.arange(seq), bp), jnp.tile(n_past + jnp.arange(dec), bs)])
    cs, sg = _rope_tables(pos)
    cache_krope_t = jnp.swapaxes(cache_krope, 2, 3)
    conv0_p = jnp.zeros((1, bp, GDN_CONV - 1, GDN_QKV), F32)
    s0_p = jnp.zeros((1, bp, GDN_HEADS, GDN_DK, GDN_DV), F32)

    o = [0]
    for n in (MLA_Q_RANK, MLA_KV_RANK, MLA_ROPE, GDN_QKV, GDN_W, GDN_HEADS, GDN_HEADS, GMLP_W, GMLP_W, N_BRANCH * D_MODEL):
        o.append(o[-1] + n)
    w_in_b = w_in.astype(BF16)
    w_in_p = jnp.concatenate([
        w_in_b[:, :, o[9]:o[10]], w_in_b[:, :, o[3]:o[4]], w_in_b[:, :, o[7]:o[8]], w_in_b[:, :, o[8]:o[9]], w_in_b[:, :, o[4]:o[5]],
        w_in_b[:, :, o[0]:o[1]], w_in_b[:, :, o[1]:o[2]], w_in_b[:, :, o[2]:o[3]], w_in_b[:, :, o[5]:o[7]],
        jnp.zeros((depth, d, Z_COLS - Z_SEG - MLA_ROPE - 2 * GDN_HEADS), BF16),
    ], axis=2)
    w_br = (w_br_mla.astype(BF16), w_br_gdn.astype(BF16), w_br_gmlp.astype(BF16))
    w_o_b = w_o.astype(BF16)
    w_gu_b = ffn_w_gu.astype(BF16)
    w_down_b = ffn_w_down.astype(BF16)

    outs = [[] for _ in range(9)]
    for l in range(depth):
        w_uq = mla_w_uq[l].reshape(MLA_Q_RANK, MLA_HEADS, MLA_NOPE + MLA_ROPE)
        w_uq = jnp.concatenate([w_uq, jnp.zeros((MLA_Q_RANK, MLA_HEADS, LANES - MLA_ROPE), F32)], axis=-1)
        w_uq = w_uq.reshape(MLA_Q_RANK, -1).astype(BF16)
        w_uk = mla_w_uk[l].reshape(MLA_KV_RANK, -1).astype(BF16)
        w_uv = mla_w_uv[l].reshape(MLA_KV_RANK, -1).astype(BF16)
        w_ukt = jnp.transpose(mla_w_uk[l], (1, 2, 0)).astype(BF16)
        w_uv_h = jnp.transpose(mla_w_uv[l], (1, 0, 2)).astype(BF16)
        g_kr_pad = _pad_lanes(mla_kr_g[l])
        g_qr_pad = _pad_lanes(mla_qr_g[l])
        a_log_row = _pad_lanes(gdn_a_log[l], SEG_A)
        dt_row = _pad_lanes(gdn_dt_bias[l], SEG_A)
        bias_full = jnp.repeat(gmlp_b_s[l].T, GMLP_GROUP_W, axis=1)
        wt_dec = jnp.repeat(jnp.transpose(gmlp_w_s[l][:, :dec, :dec], (2, 1, 0)), GMLP_GROUP_W, axis=2)

        z = norm_matmul(x, norm_mix_g[l], w_in_p, l, name="in_proj")

        q_lat, ckv, kr_pad = mla_prep(z, cs, sg, mla_q_norm_g[l], mla_kv_norm_g[l], g_kr_pad)
        qcat = q_project(q_lat, w_uq, cs, sg, mla_qn_g[l], g_qr_pad)
        kcat, vexp = kv_expand(ckv, kr_pad, w_uk, w_uv, mla_kn_g[l], mp)
        o_mla_p = mla_prompt_attention(qcat, kcat, vexp, bp, seq)
        q_abs = absorb_queries(qcat, w_ukt, mla_kn_g[l], mp, ms)
        q_abs = jnp.transpose(q_abs.reshape(MLA_HEADS, bs, dec, MLA_KV_RANK), (1, 0, 2, 3)).reshape(bs, MLA_HEADS * dec, MLA_KV_RANK)
        q_rope = qcat[mp:].reshape(bs, dec, MLA_HEADS, MLA_NOPE + LANES)[..., MLA_NOPE:MLA_NOPE + MLA_ROPE]
        q_rope = jnp.transpose(q_rope, (0, 2, 1, 3)).reshape(bs, MLA_HEADS * dec, MLA_ROPE).astype(F32)
        o_lat = mla_sample_attention(q_abs, q_rope, ckv, kr_pad, w_ukt.reshape(-1, MLA_KV_RANK), cache_ckv, cache_krope_t, page_table, l, mp, dec)
        o_lat = jnp.transpose(o_lat.reshape(bs, MLA_HEADS, dec, MLA_KV_RANK), (1, 0, 2, 3)).reshape(MLA_HEADS, ms, MLA_KV_RANK)
        o_mla_s = unabsorb_values(o_lat, w_uv_h)

        o_gdn_p, conv_p, s_p = gdn_branch(z, conv0_p, s0_p, 0, gdn_conv_w[l], a_log_row, dt_row, gdn_o_norm_g[l], 0, bp, seq)
        o_gdn_s, conv_s, s_s = gdn_branch(z, state_conv, state_gdn, l, gdn_conv_w[l], a_log_row, dt_row, gdn_o_norm_g[l], mp, bs, dec)

        o_gmlp_p = gmlp_prompt(z, gmlp_ln_g[l], gmlp_ln_b[l], gmlp_w_s[l], bias_full, mp)
        o_gmlp_s, v_rows = gmlp_sample(z, gmlp_ln_g[l], gmlp_ln_b[l], wt_dec, bias_full[:dec], mp, ms, dec)

        merged = gated_merge((o_mla_p, o_gdn_p, o_gmlp_p), (o_mla_s, o_gdn_s, o_gmlp_s), w_br, l, z)
        x = matmul_residual(merged, w_o_b, l, x, (1024, 512, 256, 128), name="out_proj")
        act = swiglu_up(x, norm_ffn_g[l], w_gu_b, l)
        x = matmul_residual(act, w_down_b, l, x, (1024, 512, 256, 128), name="ffn_down")

        kr = kr_pad[:, :MLA_ROPE]
        for i, v in enumerate((
            ckv[:mp].reshape(bp, seq, -1), kr[:mp].reshape(bp, seq, -1), s_p, conv_p,
            ckv[mp:].reshape(bs, dec, -1), kr[mp:].reshape(bs, dec, -1), s_s, conv_s, v_rows.reshape(bs, dec, -1),
        )):
            outs[i].append(v)

    return (x[:mp].reshape(bp, seq, d), x[mp:].reshape(bs, dec, d)) + tuple(jnp.stack(v, axis=0) for v in outs)
```

```python
import functools
import math

import jax
import jax.numpy as jnp
from jax import lax
from jax.experimental import pallas as pl
from jax.experimental.pallas import tpu as pltpu

F32 = jnp.float32
BF16 = jnp.bfloat16

D_MODEL = 2048
MLA_HEADS = 8
MLA_Q_RANK = 512
MLA_KV_RANK = 256
MLA_NOPE = 128
MLA_ROPE = 64
MLA_V = 128
ROPE_THETA = 10000.0
GDN_HEADS = 8
GDN_DK = 128
GDN_DV = 128
GDN_CONV = 4
GDN_CHUNK = 64
GDN_QKV = 2 * GDN_HEADS * GDN_DK + GDN_HEADS * GDN_DV
GDN_W = GDN_HEADS * GDN_DV
GMLP_GROUPS = 8
GMLP_GROUP_W = 128
GMLP_CHUNK = 128
GMLP_W = GMLP_GROUPS * GMLP_GROUP_W
N_BRANCH = 3
FFN_HIDDEN = -(-8 * D_MODEL // (3 * 256)) * 256
EPS = 1e-6

LANES = 128
VMEM_LIMIT = 56 * 1024 * 1024

Z_QLAT = 0
Z_KV = Z_QLAT + MLA_Q_RANK
Z_SEG = Z_KV + MLA_KV_RANK
ZA_COLS = 1024
Z_QKV = 0
Z_Z = Z_QKV + GDN_QKV
Z_U = 0
Z_V = Z_U + GMLP_W
Z_GATE = Z_V + GMLP_W
SEG_A = MLA_ROPE
SEG_B = MLA_ROPE + GDN_HEADS

NT = (((1,), (1,)), ((), ()))
TN = (((0,), (0,)), ((), ()))


def _pick(n, cands):
    for c in cands:
        if n % c == 0:
            return c
    raise ValueError(f"no block size in {cands} divides {n}")


def _params(*sem):
    return pltpu.CompilerParams(dimension_semantics=sem, vmem_limit_bytes=VMEM_LIMIT)


def _rms(x, n=None):
    n = x.shape[-1] if n is None else n
    return x * lax.rsqrt(jnp.sum(x * x, axis=-1, keepdims=True) * (1.0 / n) + EPS)


def _silu(x):
    return x * jax.nn.sigmoid(x)


def _gelu(x):
    return 0.5 * x * (1.0 + lax.erf(x * (1.0 / math.sqrt(2.0))))


def _rope128(y, cs, sg):
    lane = lax.broadcasted_iota(jnp.int32, y.shape, 1)
    swapped = jnp.where((lane & 63) < 32, pltpu.roll(y, 96, 1), pltpu.roll(y, 32, 1))
    return y * cs + swapped * sg


def _w_spec(k, tn, layer, shift=0):
    return pl.BlockSpec((None, k, tn), lambda i, j: (layer, 0, j + shift))


def _norm_rows(x_ref, g_ref, h_sc):
    @pl.when(pl.program_id(1) == 0)
    def _():
        h_sc[...] = (_rms(x_ref[...]) * g_ref[...]).astype(h_sc.dtype)


def _norm_mm_t_kernel(x_ref, g_ref, wt_ref, o_ref, h_sc):
    _norm_rows(x_ref, g_ref, h_sc)
    o_ref[...] = lax.dot_general(h_sc[...], wt_ref[...].astype(BF16), NT, preferred_element_type=F32)


def norm_matmul_t(x, g, w_t, layer, row0, n, name):
    m, k = x.shape
    tm = _pick(m, (1024, 512, 256, 128))
    tn = 512
    base = layer * w_t.shape[1] + row0
    w_t = w_t.reshape(-1, k)
    assert base % 16 == 0
    wt_spec = pl.BlockSpec((pl.Element(tn), pl.Element(k)), lambda i, j: (pl.multiple_of(base + j * tn, 16), 0))
    return pl.pallas_call(
        _norm_mm_t_kernel,
        grid=(m // tm, n // tn),
        in_specs=[pl.BlockSpec((tm, k), lambda i, j: (i, 0)), pl.BlockSpec((1, k), lambda i, j: (0, 0)), wt_spec],
        out_specs=pl.BlockSpec((tm, tn), lambda i, j: (i, j)),
        out_shape=jax.ShapeDtypeStruct((m, n), F32),
        scratch_shapes=[pltpu.VMEM((tm, k), BF16)],
        compiler_params=_params("parallel", "arbitrary"),
        name=name,
    )(x, g.reshape(1, k), w_t)


def _mm_res_kernel(a_ref, w_ref, x_ref, o_ref):
    o_ref[...] = x_ref[...] + jnp.dot(a_ref[...], w_ref[...], preferred_element_type=F32)


def matmul_residual(a, w, layer, residual, tm_cands, name):
    m, k = a.shape
    n = w.shape[2]
    tm = _pick(m, tm_cands)
    tn = 512
    return pl.pallas_call(
        _mm_res_kernel,
        grid=(m // tm, n // tn),
        in_specs=[pl.BlockSpec((tm, k), lambda i, j: (i, 0)), _w_spec(k, tn, layer), pl.BlockSpec((tm, tn), lambda i, j: (i, j))],
        out_specs=pl.BlockSpec((tm, tn), lambda i, j: (i, j)),
        out_shape=jax.ShapeDtypeStruct((m, n), F32),
        compiler_params=_params("parallel", "arbitrary"),
        name=name,
    )(a, w, residual)


def _swiglu_kernel(x_ref, g_ref, wg_ref, wu_ref, o_ref, h_sc):
    _norm_rows(x_ref, g_ref, h_sc)
    h = h_sc[...]
    gate = jnp.dot(h, wg_ref[...].astype(BF16), preferred_element_type=F32)
    up = jnp.dot(h, wu_ref[...].astype(BF16), preferred_element_type=F32)
    o_ref[...] = (_silu(gate) * up).astype(o_ref.dtype)


def swiglu_up(x, g, w_gu, layer):
    m, k = x.shape
    hid = w_gu.shape[2] // 2
    tm = _pick(m, (1024, 512, 256, 128))
    tn = 512
    nb = hid // tn
    return pl.pallas_call(
        _swiglu_kernel,
        grid=(m // tm, nb),
        in_specs=[
            pl.BlockSpec((tm, k), lambda i, j: (i, 0)), pl.BlockSpec((1, k), lambda i, j: (0, 0)),
            _w_spec(k, tn, layer), _w_spec(k, tn, layer, shift=nb),
        ],
        out_specs=pl.BlockSpec((tm, tn), lambda i, j: (i, j)),
        out_shape=jax.ShapeDtypeStruct((m, hid), BF16),
        scratch_shapes=[pltpu.VMEM((tm, k), BF16)],
        compiler_params=_params("parallel", "arbitrary"),
        name="swiglu_up",
    )(x, g.reshape(1, k), w_gu, w_gu)


def _merge_kernel(*refs, prompt_blocks):
    a_prompt, a_sample, w_refs, g_refs, o_ref = refs[0:3], refs[3:6], refs[6:9], refs[9:12], refs[12]
    i = pl.program_id(0)

    def run(a_refs):
        acc = None
        for a_ref, w_ref, g_ref in zip(a_refs, w_refs, g_refs):
            y = jnp.dot(a_ref[...].astype(BF16), w_ref[...], preferred_element_type=F32)
            y = jax.nn.sigmoid(g_ref[...]) * y
            acc = y if acc is None else acc + y
        o_ref[...] = acc.astype(o_ref.dtype)

    @pl.when(i < prompt_blocks)
    def _():
        run(a_prompt)

    @pl.when(i >= prompt_blocks)
    def _():
        run(a_sample)


def gated_merge(branches_prompt, branches_sample, weights, layer, z):
    mp, k = branches_prompt[0].shape
    ms = branches_sample[0].shape[0]
    n = weights[0].shape[2]
    tm = _pick(math.gcd(mp, ms), (1024, 512, 256, 128))
    tn = 256
    nb = n // tn
    npb = mp // tm
    p_spec = pl.BlockSpec((tm, k), lambda i, j: (jnp.minimum(i, npb - 1), 0))
    s_spec = pl.BlockSpec((tm, k), lambda i, j: (jnp.maximum(i - npb, 0), 0))
    w_spec = _w_spec(k, tn, layer)
    g_specs = [pl.BlockSpec((tm, tn), functools.partial(lambda i, j, b: (i, Z_GATE // tn + b * nb + j), b=b)) for b in range(N_BRANCH)]
    return pl.pallas_call(
        functools.partial(_merge_kernel, prompt_blocks=npb),
        grid=((mp + ms) // tm, nb),
        in_specs=[p_spec] * 3 + [s_spec] * 3 + [w_spec] * 3 + g_specs,
        out_specs=pl.BlockSpec((tm, tn), lambda i, j: (i, j)),
        out_shape=jax.ShapeDtypeStruct((mp + ms, n), BF16),
        compiler_params=_params("parallel", "arbitrary"),
        name="gated_merge",
    )(*branches_prompt, *branches_sample, *weights, z, z, z)


def _mla_prep_kernel(ql_ref, kv_ref, seg_ref, cs_ref, sg_ref, gq_ref, gkv_ref, gkr_ref, qo_ref, ckv_ref, kr_ref):
    qo_ref[...] = (_rms(ql_ref[...]) * gq_ref[...]).astype(qo_ref.dtype)
    ckv_ref[...] = _rms(kv_ref[...]) * gkv_ref[...]
    seg = seg_ref[...]
    lane = lax.broadcasted_iota(jnp.int32, seg.shape, 1)
    x = jnp.where(lane < MLA_ROPE, seg, 0.0)
    kr_ref[...] = _rope128(_rms(x, MLA_ROPE) * gkr_ref[...], cs_ref[...], sg_ref[...])


def mla_prep(z, cs, sg, g_q, g_kv, g_kr_pad):
    m = z.shape[0]
    tm = _pick(m, (512, 256, 128))
    row = lambda w, c: pl.BlockSpec((tm, w), lambda i: (i, c))
    par = lambda w: pl.BlockSpec((1, w), lambda i: (0, 0))
    return pl.pallas_call(
        _mla_prep_kernel,
        grid=(m // tm,),
        in_specs=[
            row(MLA_Q_RANK, Z_QLAT // MLA_Q_RANK), row(MLA_KV_RANK, Z_KV // MLA_KV_RANK), row(LANES, Z_SEG // LANES),
            row(LANES, 0), row(LANES, 0), par(MLA_Q_RANK), par(MLA_KV_RANK), par(LANES),
        ],
        out_specs=[row(MLA_Q_RANK, 0), row(MLA_KV_RANK, 0), row(LANES, 0)],
        out_shape=[
            jax.ShapeDtypeStruct((m, MLA_Q_RANK), BF16),
            jax.ShapeDtypeStruct((m, MLA_KV_RANK), F32),
            jax.ShapeDtypeStruct((m, LANES), F32),
        ],
        compiler_params=_params("parallel"),
        name="mla_prep",
    )(z, z, z, cs, sg, g_q.reshape(1, -1), g_kv.reshape(1, -1), g_kr_pad)


def _q_kernel(a_ref, w_ref, cs_ref, sg_ref, gn_ref, gr_ref, o_ref, *, scale):
    a = a_ref[...]
    cs, sg = cs_ref[...], sg_ref[...]
    hw = MLA_NOPE + LANES
    q = jnp.dot(a, w_ref[...], preferred_element_type=F32)
    for h in range(MLA_HEADS):
        qn = q[:, h * hw:h * hw + MLA_NOPE]
        o_ref[:, h * hw:h * hw + MLA_NOPE] = (_rms(qn) * gn_ref[...] * scale).astype(o_ref.dtype)
        qr = q[:, h * hw + MLA_NOPE:(h + 1) * hw]
        qr = _rope128(_rms(qr, MLA_ROPE) * gr_ref[...], cs, sg)
        o_ref[:, h * hw + MLA_NOPE:(h + 1) * hw] = (qr * scale).astype(o_ref.dtype)


def q_project(q_lat, w_uq_pad, cs, sg, g_qn, g_qr_pad):
    m, k = q_lat.shape
    n = w_uq_pad.shape[1]
    tm = _pick(m, (512, 256, 128))
    scale = (MLA_NOPE + MLA_ROPE) ** -0.5
    return pl.pallas_call(
        functools.partial(_q_kernel, scale=scale),
        grid=(m // tm,),
        in_specs=[
            pl.BlockSpec((tm, k), lambda i: (i, 0)), pl.BlockSpec((k, n), lambda i: (0, 0)),
            pl.BlockSpec((tm, LANES), lambda i: (i, 0)), pl.BlockSpec((tm, LANES), lambda i: (i, 0)),
            pl.BlockSpec((1, LANES), lambda i: (0, 0)), pl.BlockSpec((1, LANES), lambda i: (0, 0)),
        ],
        out_specs=pl.BlockSpec((tm, n), lambda i: (i, 0)),
        out_shape=jax.ShapeDtypeStruct((m, n), BF16),
        compiler_params=_params("parallel"),
        name="q_project",
    )(q_lat, w_uq_pad, cs, sg, g_qn.reshape(1, -1), g_qr_pad)


def _kv_kernel(a_ref, wk_ref, wv_ref, kr_ref, gk_ref, ko_ref, vo_ref):
    a = a_ref[...].astype(BF16)
    krb = kr_ref[...].astype(BF16)
    hw = MLA_NOPE + LANES
    kn_all = jnp.dot(a, wk_ref[...], preferred_element_type=F32)
    for h in range(MLA_HEADS):
        kn = kn_all[:, h * MLA_NOPE:(h + 1) * MLA_NOPE]
        ko_ref[:, h * hw:h * hw + MLA_NOPE] = (_rms(kn) * gk_ref[...]).astype(ko_ref.dtype)
        ko_ref[:, h * hw + MLA_NOPE:(h + 1) * hw] = krb
    vo_ref[...] = jnp.dot(a, wv_ref[...], preferred_element_type=F32).astype(vo_ref.dtype)


def kv_expand(ckv, kr_pad, w_uk, w_uv, g_kn, rows):
    tm = _pick(rows, (512, 256, 128))
    hw = MLA_NOPE + LANES
    return pl.pallas_call(
        _kv_kernel,
        grid=(rows // tm,),
        in_specs=[
            pl.BlockSpec((tm, MLA_KV_RANK), lambda i: (i, 0)),
            pl.BlockSpec(w_uk.shape, lambda i: (0, 0)), pl.BlockSpec(w_uv.shape, lambda i: (0, 0)),
            pl.BlockSpec((tm, LANES), lambda i: (i, 0)), pl.BlockSpec((1, MLA_NOPE), lambda i: (0, 0)),
        ],
        out_specs=[pl.BlockSpec((tm, MLA_HEADS * hw), lambda i: (i, 0)), pl.BlockSpec((tm, MLA_HEADS * MLA_V), lambda i: (i, 0))],
        out_shape=[jax.ShapeDtypeStruct((rows, MLA_HEADS * hw), BF16), jax.ShapeDtypeStruct((rows, MLA_HEADS * MLA_V), BF16)],
        compiler_params=_params("parallel"),
        name="kv_expand",
    )(ckv, w_uk, w_uv, kr_pad, g_kn.reshape(1, -1))


def _flash_kernel(q_ref, k_ref, v_ref, o_ref, *, tq, nq, heads):
    qi = pl.program_id(2)
    hw = MLA_NOPE + LANES
    for i in range(nq):
        @pl.when(qi == i)
        def _(i=i):
            kend = (i + 1) * tq
            s = [lax.dot_general(q_ref[:, h * hw:(h + 1) * hw], k_ref[0:kend, h * hw:(h + 1) * hw], NT, preferred_element_type=F32)
                 for h in range(heads)]
            row = i * tq + lax.broadcasted_iota(jnp.int32, s[0].shape, 0)
            col = lax.broadcasted_iota(jnp.int32, s[0].shape, 1)
            p, l = [], []
            for h in range(heads):
                sh = jnp.where(col <= row, s[h], -jnp.inf)
                ph = jnp.exp(sh - jnp.max(sh, axis=-1, keepdims=True))
                l.append(jnp.sum(ph, axis=-1, keepdims=True))
                p.append(ph.astype(BF16))
            o = [jnp.dot(p[h], v_ref[0:kend, h * MLA_V:(h + 1) * MLA_V], preferred_element_type=F32) for h in range(heads)]
            for h in range(heads):
                o_ref[:, h * MLA_V:(h + 1) * MLA_V] = (o[h] / l[h]).astype(o_ref.dtype)


def mla_prompt_attention(qcat, kcat, v, batch, seq):
    tq = _pick(seq, (256, 128))
    nq = seq // tq
    heads = 4
    hw = heads * (MLA_NOPE + LANES)
    vw = heads * MLA_V
    return pl.pallas_call(
        functools.partial(_flash_kernel, tq=tq, nq=nq, heads=heads),
        grid=(batch, MLA_HEADS // heads, nq),
        in_specs=[
            pl.BlockSpec((tq, hw), lambda b, h, i: (b * nq + i, h)),
            pl.BlockSpec((seq, hw), lambda b, h, i: (b, h)),
            pl.BlockSpec((seq, vw), lambda b, h, i: (b, h)),
        ],
        out_specs=pl.BlockSpec((tq, vw), lambda b, h, i: (b * nq + i, h)),
        out_shape=jax.ShapeDtypeStruct((batch * seq, MLA_HEADS * MLA_V), BF16),
        compiler_params=_params("parallel", "parallel", "arbitrary"),
        name="mla_prompt_attention",
    )(qcat, kcat, v)


def _absorb_kernel(q_ref, w_ref, g_ref, o_ref):
    q = q_ref[:, 0:MLA_NOPE].astype(F32) * g_ref[...]
    o_ref[0] = jnp.dot(q.astype(BF16), w_ref[0], preferred_element_type=F32)


def absorb_queries(qcat, w_ukt, g_kn, row0, rows):
    hw = MLA_NOPE + LANES
    return pl.pallas_call(
        _absorb_kernel,
        grid=(MLA_HEADS,),
        in_specs=[
            pl.BlockSpec((rows, hw), lambda h: (row0 // rows, h)),
            pl.BlockSpec((1, MLA_NOPE, MLA_KV_RANK), lambda h: (h, 0, 0)),
            pl.BlockSpec((1, MLA_NOPE), lambda h: (0, 0)),
        ],
        out_specs=pl.BlockSpec((1, rows, MLA_KV_RANK), lambda h: (h, 0, 0)),
        out_shape=jax.ShapeDtypeStruct((MLA_HEADS, rows, MLA_KV_RANK), F32),
        compiler_params=_params("parallel"),
        name="absorb_queries",
    )(qcat, w_ukt, g_kn.reshape(1, -1))


def _unabsorb_kernel(a_ref, w_ref, o_ref):
    o_ref[...] = jnp.dot(a_ref[0].astype(BF16), w_ref[0], preferred_element_type=F32).astype(o_ref.dtype)


def unabsorb_values(o_lat, w_uv_h):
    _, rows, r = o_lat.shape
    return pl.pallas_call(
        _unabsorb_kernel,
        grid=(MLA_HEADS,),
        in_specs=[pl.BlockSpec((1, rows, r), lambda h: (h, 0, 0)), pl.BlockSpec((1, r, MLA_V), lambda h: (h, 0, 0))],
        out_specs=pl.BlockSpec((rows, MLA_V), lambda h: (0, h)),
        out_shape=jax.ShapeDtypeStruct((rows, MLA_HEADS * MLA_V), BF16),
        compiler_params=_params("parallel"),
        name="unabsorb_values",
    )(o_lat, w_uv_h)


def _sattn_kernel(pt_ref, qa_ref, qr_ref, cn_ref, kn_ref, wk_ref, ckv_hbm, krt_hbm, o_ref,
                  ckv_buf, krt_buf, ckb, s_sc, sem, *, layer, tile_pages, n_pages, page, dec):
    b = pl.program_id(0)
    nb = pl.num_programs(0)
    slot = b % 2
    ntiles = n_pages // tile_pages
    tk = tile_pages * page

    def tile_copies(bb, j, sl):
        out = []
        for g in range(tile_pages):
            slot_page = j * tile_pages + g
            pg = pt_ref[bb * n_pages + slot_page]
            out.append(pltpu.make_async_copy(ckv_hbm.at[layer, pg], ckv_buf.at[sl, slot_page], sem.at[sl, 0]))
            out.append(pltpu.make_async_copy(krt_hbm.at[layer, pg], krt_buf.at[sl, slot_page], sem.at[sl, 1]))
        return out

    @pl.when(b == 0)
    def _():
        def first(j, carry):
            for cp in tile_copies(b, j, slot):
                cp.start()
            return carry
        lax.fori_loop(0, ntiles, first, 0)

    pltpu.make_async_copy(ckv_hbm.at[layer, pl.ds(0, n_pages)], ckv_buf.at[slot], sem.at[slot, 0]).wait()
    pltpu.make_async_copy(krt_hbm.at[layer, pl.ds(0, n_pages)], krt_buf.at[slot], sem.at[slot, 1]).wait()

    qa = qa_ref[0].astype(BF16)
    qr = qr_ref[0].astype(BF16)

    def nope_scores(ck):
        kn = lax.dot_general(wk_ref[...], ck, NT, preferred_element_type=F32)
        raw = lax.dot_general(qa, ck, NT, preferred_element_type=F32)
        inv = []
        for h in range(MLA_HEADS):
            kh = kn[h * MLA_NOPE:(h + 1) * MLA_NOPE]
            r = lax.rsqrt(jnp.sum(kh * kh, axis=0, keepdims=True) * (1.0 / MLA_NOPE) + EPS)
            inv.append(jnp.broadcast_to(r, (dec, r.shape[1])))
        return raw * jnp.concatenate(inv, axis=0)

    def tile(j, carry):
        @pl.when(b + 1 < nb)
        def _():
            for cp in tile_copies(b + 1, j, 1 - slot):
                cp.start()

        r0 = pl.multiple_of(j * tk, tk)
        p0 = j * tile_pages
        ck = ckv_buf[slot, pl.ds(p0, tile_pages)].reshape(tk, ckv_buf.shape[-1]).astype(BF16)
        ckb[pl.ds(r0, tk), :] = ck
        krt = jnp.concatenate([krt_buf[slot, p0 + g] for g in range(tile_pages)], axis=1).astype(BF16)
        rope = jnp.dot(qr, krt, preferred_element_type=F32)
        s_sc[j] = nope_scores(ck) + rope
        return carry
    lax.fori_loop(0, ntiles, tile, 0)

    pad = page - dec
    cnew = jnp.concatenate([cn_ref[...], jnp.zeros((pad, cn_ref.shape[1]), F32)], axis=0).astype(BF16)
    knew = jnp.concatenate([kn_ref[:, 0:MLA_ROPE], jnp.zeros((pad, MLA_ROPE), F32)], axis=0).astype(BF16)
    s_new = nope_scores(cnew) + lax.dot_general(qr, knew, NT, preferred_element_type=F32)
    qpos = lax.broadcasted_iota(jnp.int32, s_new.shape, 0) % dec
    kpos = lax.broadcasted_iota(jnp.int32, s_new.shape, 1)
    s_new = jnp.where(kpos <= qpos, s_new, -jnp.inf)

    s_max = s_sc[0]
    for j in range(1, ntiles):
        s_max = jnp.maximum(s_max, s_sc[j])
    m = jnp.maximum(jnp.max(s_new, axis=-1, keepdims=True), jnp.max(s_max, axis=-1, keepdims=True))
    p_new = jnp.exp(s_new - m)
    p = [jnp.exp(s_sc[j] - m) for j in range(ntiles)]
    p_sum = p[0]
    for pj in p[1:]:
        p_sum = p_sum + pj
    l = jnp.sum(p_new, axis=-1, keepdims=True) + jnp.sum(p_sum, axis=-1, keepdims=True)
    p_all = jnp.concatenate([pj.astype(BF16) for pj in p], axis=1)
    acc = jnp.dot(p_all, ckb[...], preferred_element_type=F32) + jnp.dot(p_new.astype(BF16), cnew, preferred_element_type=F32)
    o_ref[0] = acc / l


def mla_sample_attention(q_abs, q_rope, ckv, kr_pad, w_ukt2, cache_ckv, cache_krope_t, page_table, layer, row0, dec):
    nb, nq, r = q_abs.shape
    n_pages = page_table.shape[1]
    page = cache_ckv.shape[2]
    tile_pages = _pick(n_pages, (8, 4, 2, 1))
    ntiles = n_pages // tile_pages
    tk = tile_pages * page
    grid_spec = pltpu.PrefetchScalarGridSpec(
        num_scalar_prefetch=1,
        grid=(nb,),
        in_specs=[
            pl.BlockSpec((1, nq, r), lambda b, pt: (b, 0, 0)),
            pl.BlockSpec((1, nq, MLA_ROPE), lambda b, pt: (b, 0, 0)),
            pl.BlockSpec((dec, r), lambda b, pt: (row0 // dec + b, 0)),
            pl.BlockSpec((dec, LANES), lambda b, pt: (row0 // dec + b, 0)),
            pl.BlockSpec(w_ukt2.shape, lambda b, pt: (0, 0)),
            pl.BlockSpec(memory_space=pl.ANY),
            pl.BlockSpec(memory_space=pl.ANY),
        ],
        out_specs=pl.BlockSpec((1, nq, r), lambda b, pt: (b, 0, 0)),
        scratch_shapes=[
            pltpu.VMEM((2, n_pages, page, r), F32),
            pltpu.VMEM((2, n_pages, MLA_ROPE, page), F32),
            pltpu.VMEM((n_pages * page, r), BF16),
            pltpu.VMEM((ntiles, nq, tk), F32),
            pltpu.SemaphoreType.DMA((2, 2)),
        ],
    )
    return pl.pallas_call(
        functools.partial(_sattn_kernel, layer=layer, tile_pages=tile_pages, n_pages=n_pages, page=page, dec=dec),
        grid_spec=grid_spec,
        out_shape=jax.ShapeDtypeStruct((nb, nq, r), F32),
        compiler_params=_params("arbitrary"),
        name="mla_sample_attention",
    )(page_table.reshape(-1), q_abs, q_rope, ckv, kr_pad, w_ukt2, cache_ckv, cache_krope_t)


def _chunk_masks(c):
    ii = lax.broadcasted_iota(jnp.int32, (c, c), 0)
    jj = lax.broadcasted_iota(jnp.int32, (c, c), 1)
    return jj <= ii, jj < ii


def _split_bf16(x, terms):
    parts = []
    for _ in range(terms):
        p = x.astype(BF16)
        parts.append(p)
        x = x - p.astype(F32)
    return parts


def _dot_split(a, b):
    ah, al = _split_bf16(a, 2)
    bh, bl = _split_bf16(b, 2)
    dot = functools.partial(jnp.dot, preferred_element_type=F32)
    return dot(ah, bh) + (dot(ah, bl) + dot(al, bh))


def _gdn_prep_kernel(x_ref, seg_ref, c0_ref, cw_ref, al_ref, dt_ref, qkv_ref, gb_ref, cn_ref, a_ref, ext, *, nsq, rows, chunk):
    t = pl.program_id(1)
    hk = GDN_HEADS * GDN_DK
    tail = GDN_CONV - 1
    tb = nsq * rows
    cw = cw_ref[...]
    for sq in range(nsq):
        @pl.when(t == 0)
        def _(sq=sq):
            ext[sq, 8 - tail:8, :] = c0_ref[sq]

        ext[sq, 8:8 + rows, :] = x_ref[sq * rows:(sq + 1) * rows, :]
        y = cw[0:1] * ext[sq, 5:5 + rows, :]
        for j in range(1, GDN_CONV):
            y = y + cw[j:j + 1] * ext[sq, 5 + j:5 + j + rows, :]
        new_tail = ext[sq, 8 + rows - tail:8 + rows, :]
        cn_ref[sq] = new_tail
        ext[sq, 8 - tail:8, :] = new_tail
        qkv_ref[sq * rows:(sq + 1) * rows, :] = _silu(y)

    seg = seg_ref[...]
    lane = lax.broadcasted_iota(jnp.int32, seg.shape, 1)
    beta = jax.nn.sigmoid(seg)
    g = -jnp.exp(al_ref[...]) * jax.nn.softplus(seg + dt_ref[...])
    ri = lax.broadcasted_iota(jnp.int32, (tb, tb), 0)
    ci = lax.broadcasted_iota(jnp.int32, (tb, tb), 1)
    cum = jnp.where((ri // chunk == ci // chunk) & (ci <= ri), 1.0, 0.0).astype(BF16)
    gcum = sum(jnp.dot(cum, p, preferred_element_type=F32) for p in reversed(_split_bf16(g, 3)))
    gb = jnp.where((lane >= SEG_A) & (lane < SEG_B), gcum, beta)
    gb_ref[...] = gb
    gcum_t = gcum.T

    _, strict = _chunk_masks(chunk)
    for h in range(GDN_HEADS):
        qh = qkv_ref[:, h * GDN_DK:(h + 1) * GDN_DK]
        qh = qh * lax.rsqrt(jnp.sum(qh * qh, axis=-1, keepdims=True) + EPS) * (GDN_DK ** -0.5)
        kh = qkv_ref[:, hk + h * GDN_DK:hk + (h + 1) * GDN_DK]
        kh = kh * lax.rsqrt(jnp.sum(kh * kh, axis=-1, keepdims=True) + EPS)
        qkv_ref[:, h * GDN_DK:(h + 1) * GDN_DK] = qh
        qkv_ref[:, hk + h * GDN_DK:hk + (h + 1) * GDN_DK] = kh
        kb = (kh * beta[:, SEG_B + h:SEG_B + h + 1]).astype(BF16)
        kh = kh.astype(BF16)
        for c in range(tb // chunk):
            r0, r1 = c * chunk, (c + 1) * chunk
            gc = gcum[r0:r1, SEG_A + h:SEG_A + h + 1]
            gr = gcum_t[SEG_A + h:SEG_A + h + 1, r0:r1]
            decay = jnp.exp(jnp.where(strict, gc - gr, 0.0))
            kk = lax.dot_general(kb[r0:r1], kh[r0:r1], NT, preferred_element_type=F32)
            a_ref[c, h] = jnp.where(strict, kk * decay, 0.0)


def gdn_prep(z_seg, z_qkv, conv0, layer, conv_w, a_log_row, dt_row, row0, nseq, seqlen, nsq, rows, chunk):
    nt = seqlen // rows
    tb = nsq * rows
    nc = tb // chunk
    base = row0 // tb
    assert nsq == 1 or nt == 1
    return pl.pallas_call(
        functools.partial(_gdn_prep_kernel, nsq=nsq, rows=rows, chunk=chunk),
        grid=(nseq // nsq, nt),
        in_specs=[
            pl.BlockSpec((tb, GDN_QKV), lambda b, t: (base + b * nt + t, Z_QKV // GDN_QKV)),
            pl.BlockSpec((tb, LANES), lambda b, t: (base + b * nt + t, Z_SEG // LANES)),
            pl.BlockSpec((None, nsq, GDN_CONV - 1, GDN_QKV), lambda b, t: (layer, b, 0, 0)),
            pl.BlockSpec((GDN_CONV, GDN_QKV), lambda b, t: (0, 0)),
            pl.BlockSpec((1, LANES), lambda b, t: (0, 0)),
            pl.BlockSpec((1, LANES), lambda b, t: (0, 0)),
        ],
        out_specs=[
            pl.BlockSpec((tb, GDN_QKV), lambda b, t: (b * nt + t, 0)),
            pl.BlockSpec((tb, LANES), lambda b, t: (b * nt + t, 0)),
            pl.BlockSpec((nsq, GDN_CONV - 1, GDN_QKV), lambda b, t: (b, 0, 0)),
            pl.BlockSpec((nc, GDN_HEADS, chunk, chunk), lambda b, t: (b * nt + t, 0, 0, 0)),
        ],
        out_shape=[
            jax.ShapeDtypeStruct((nseq * seqlen, GDN_QKV), F32),
            jax.ShapeDtypeStruct((nseq * seqlen, LANES), F32),
            jax.ShapeDtypeStruct((nseq, GDN_CONV - 1, GDN_QKV), F32),
            jax.ShapeDtypeStruct((nseq * seqlen // chunk, GDN_HEADS, chunk, chunk), F32),
        ],
        scratch_shapes=[pltpu.VMEM((nsq, rows + 8, GDN_QKV), F32)],
        compiler_params=_params("parallel", "arbitrary"),
        name="gdn_prep",
    )(z_qkv, z_seg, conv0, conv_w, a_log_row, dt_row)


def _trinv_kernel(a_ref, o_ref, t_sc, *, n, kb):
    i = pl.program_id(0)
    for b in range(n // kb):
        k0 = b * kb

        def body(j, acc, k0=k0):
            return acc + a_ref[0, j][None] * t_sc[j, k0:k0 + kb]

        acc = lax.fori_loop(k0, i, body, jnp.zeros((kb,) + t_sc.shape[2:], F32))
        kidx = k0 + lax.broadcasted_iota(jnp.int32, acc.shape, 0)
        row = jnp.where(kidx == i, 1.0, 0.0) - acc
        t_sc[i, k0:k0 + kb] = row
        o_ref[0, k0:k0 + kb] = row


def unit_lower_inverse(a):
    nsys, n, _ = a.shape
    tile = 8 * LANES
    npad = -(-nsys // tile) * tile
    at = jnp.transpose(a, (1, 2, 0))
    if npad != nsys:
        at = jnp.pad(at, ((0, 0), (0, 0), (0, npad - nsys)))
    at = at.reshape(n, n, npad // LANES, LANES)
    kb = min(n, 16)
    outs = []
    for s in range(npad // tile):
        blk = at[:, :, s * 8:(s + 1) * 8]
        outs.append(pl.pallas_call(
            functools.partial(_trinv_kernel, n=n, kb=kb),
            grid=(n,),
            in_specs=[pl.BlockSpec((1, n, 8, LANES), lambda i: (i, 0, 0, 0))],
            out_specs=pl.BlockSpec((1, n, 8, LANES), lambda i: (i, 0, 0, 0)),
            out_shape=jax.ShapeDtypeStruct((n, n, 8, LANES), F32),
            scratch_shapes=[pltpu.VMEM((n, n, 8, LANES), F32)],
            compiler_params=_params("arbitrary"),
            name="unit_lower_inverse",
        )(blk))
    t = outs[0] if len(outs) == 1 else jnp.concatenate(outs, axis=2)
    t = t.reshape(n, n, npad)[:, :, :nsys]
    return jnp.transpose(t, (2, 0, 1))


def _gdn_scan_kernel(qkv_ref, gb_ref, z_ref, ti_ref, s0_ref, gon_ref, o_ref, so_ref, s_sc, *, nsq, rows, chunk):
    t = pl.program_id(1)
    hk = GDN_HEADS * GDN_DK

    @pl.when(t == 0)
    def _():
        s_sc[...] = s0_ref[...]

    gb = gb_ref[...]
    gb_t = gb.T
    incl, _ = _chunk_masks(chunk)
    ncs = rows // chunk
    pairs = [(sq, h) for sq in range(nsq) for h in range(GDN_HEADS)]
    dot = functools.partial(jnp.dot, preferred_element_type=F32)
    for c in range(ncs):
        span = {(sq, h): (sq * rows + c * chunk, sq * rows + (c + 1) * chunk) for sq, h in pairs}
        st1, st2, st3, st4 = {}, {}, {}, {}
        for sq, h in pairs:
            r0, r1 = span[sq, h]
            gc = gb[r0:r1, SEG_A + h:SEG_A + h + 1]
            gr = gb_t[SEG_A + h:SEG_A + h + 1, r0:r1]
            beta = gb[r0:r1, SEG_B + h:SEG_B + h + 1]
            q = qkv_ref[r0:r1, h * GDN_DK:(h + 1) * GDN_DK]
            k = qkv_ref[r0:r1, hk + h * GDN_DK:hk + (h + 1) * GDN_DK]
            v = qkv_ref[r0:r1, 2 * hk + h * GDN_DV:2 * hk + (h + 1) * GDN_DV]
            egc = jnp.exp(gc)
            glast = gc[chunk - 1:chunk, :]
            th, tl = _split_bf16(ti_ref[sq * ncs + c, h], 2)
            rhs = jnp.concatenate([v * beta, k * (beta * egc)], axis=1)
            rh, rl = _split_bf16(rhs, 2)
            sol = (dot(th, rh), dot(th, rl), dot(tl, rh))
            qk = lax.dot_general(q.astype(BF16), k.astype(BF16), NT, preferred_element_type=F32)
            decay = jnp.exp(jnp.where(incl, gc - gr, 0.0))
            k_tail = (k * jnp.exp(glast - gc)).astype(BF16)
            st1[sq, h] = (sol, qk, decay, q * egc, k_tail, jnp.exp(glast))
        for sq, h in pairs:
            sol, qk, decay, qe, k_tail, dec_last = st1[sq, h]
            uw = sol[0] + (sol[1] + sol[2])
            wq = jnp.concatenate([uw[:, GDN_DV:], qe], axis=0).astype(BF16)
            attn = jnp.where(incl, qk * decay, 0.0).astype(BF16)
            st2[sq, h] = (uw[:, :GDN_DV], wq, attn, k_tail, dec_last)
        for sq, h in pairs:
            u, wq, attn, k_tail, dec_last = st2[sq, h]
            s_h = s_sc[sq, h]
            st3[sq, h] = (u, dot(wq, s_h.astype(BF16)), attn, k_tail, s_h * dec_last)
        for sq, h in pairs:
            u, ws, attn, k_tail, s_dec = st3[sq, h]
            v_new_b = (u - ws[:chunk]).astype(BF16)
            st4[sq, h] = ws[chunk:] + dot(attn, v_new_b)
            s_sc[sq, h] = s_dec + lax.dot_general(k_tail, v_new_b, TN, preferred_element_type=F32)
        for sq, h in pairs:
            r0, r1 = span[sq, h]
            zg = z_ref[r0:r1, h * GDN_DV:(h + 1) * GDN_DV]
            o_ref[r0:r1, h * GDN_DV:(h + 1) * GDN_DV] = (_rms(st4[sq, h]) * gon_ref[...] * _silu(zg)).astype(o_ref.dtype)

    @pl.when(t == pl.num_programs(1) - 1)
    def _():
        so_ref[...] = s_sc[...]


def gdn_scan(qkv, gb, z, tinv, s0, layer, g_onorm, row0, nseq, seqlen, nsq, rows, chunk, out_dtype):
    nt = seqlen // rows
    tb = nsq * rows
    nc = tb // chunk
    base = row0 // tb
    state = (nsq, GDN_HEADS, GDN_DK, GDN_DV)
    return pl.pallas_call(
        functools.partial(_gdn_scan_kernel, nsq=nsq, rows=rows, chunk=chunk),
        grid=(nseq // nsq, nt),
        in_specs=[
            pl.BlockSpec((tb, GDN_QKV), lambda b, t: (b * nt + t, 0)),
            pl.BlockSpec((tb, LANES), lambda b, t: (b * nt + t, 0)),
            pl.BlockSpec((tb, GDN_W), lambda b, t: (base + b * nt + t, Z_Z // GDN_W)),
            pl.BlockSpec((nc, GDN_HEADS, chunk, chunk), lambda b, t: (b * nt + t, 0, 0, 0)),
            pl.BlockSpec((None,) + state, lambda b, t: (layer, b, 0, 0, 0)),
            pl.BlockSpec((1, GDN_DV), lambda b, t: (0, 0)),
        ],
        out_specs=[
            pl.BlockSpec((tb, GDN_W), lambda b, t: (b * nt + t, 0)),
            pl.BlockSpec(state, lambda b, t: (b, 0, 0, 0)),
        ],
        out_shape=[
            jax.ShapeDtypeStruct((nseq * seqlen, GDN_W), out_dtype),
            jax.ShapeDtypeStruct((nseq, GDN_HEADS, GDN_DK, GDN_DV), F32),
        ],
        scratch_shapes=[pltpu.VMEM(state, F32)],
        compiler_params=_params("parallel", "arbitrary"),
        name="gdn_scan",
    )(qkv, gb, z, tinv, s0, g_onorm.reshape(1, -1))


def gdn_branch(z_seg, z_gdn, conv0, s0, layer, conv_w, a_log_row, dt_row, g_onorm, row0, nseq, seqlen):
    chunk = min(GDN_CHUNK, seqlen)
    rows = _pick(seqlen, (256, 128, 64, 8))
    nsq = _pick(nseq, (8, 4, 2, 1)) if rows == seqlen and rows < GDN_CHUNK else 1
    out_dtype = BF16 if chunk % 16 == 0 else F32
    qkv, gb, conv_new, a = gdn_prep(z_seg, z_gdn, conv0, layer, conv_w, a_log_row, dt_row, row0, nseq, seqlen, nsq, rows, chunk)
    tinv = unit_lower_inverse(a.reshape(-1, chunk, chunk)).reshape(a.shape)
    o, s_new = gdn_scan(qkv, gb, z_gdn, tinv, s0, layer, g_onorm, row0, nseq, seqlen, nsq, rows, chunk, out_dtype)
    return o, conv_new, s_new


def _layer_norm(v, g, b):
    mu = jnp.mean(v, axis=-1, keepdims=True)
    d = v - mu
    return d * lax.rsqrt(jnp.mean(d * d, axis=-1, keepdims=True) + EPS) * g + b


def _gmlp_prompt_kernel(u_ref, v_ref, lg_ref, lb_ref, w_ref, bias_ref, o_ref, *, rows):
    vn = _layer_norm(_gelu(v_ref[...]), lg_ref[...], lb_ref[...]).astype(BF16)
    u = _gelu(u_ref[...])
    tril, _ = _chunk_masks(GMLP_CHUNK)
    bias = bias_ref[...]
    blocks = [(g, c) for g in range(GMLP_GROUPS) for c in range(rows // GMLP_CHUNK)]
    wg = [jnp.where(tril, w_ref[g], 0.0).astype(BF16) for g in range(GMLP_GROUPS)]
    s = {}
    for g, c in blocks:
        s[g, c] = jnp.dot(wg[g], vn[c * GMLP_CHUNK:(c + 1) * GMLP_CHUNK, g * GMLP_GROUP_W:(g + 1) * GMLP_GROUP_W], preferred_element_type=F32)
    for g, c in blocks:
        r0, r1 = c * GMLP_CHUNK, (c + 1) * GMLP_CHUNK
        c0, c1 = g * GMLP_GROUP_W, (g + 1) * GMLP_GROUP_W
        o_ref[r0:r1, c0:c1] = (u[r0:r1, c0:c1] * (s[g, c] + bias[:, c0:c1])).astype(o_ref.dtype)


def gmlp_prompt(z, ln_g, ln_b, w_s, bias_full, nrows):
    rows = _pick(nrows, (512, 256, 128))
    row = lambda cb: pl.BlockSpec((rows, GMLP_W), lambda i: (i, cb))
    par = lambda: pl.BlockSpec((1, GMLP_W), lambda i: (0, 0))
    return pl.pallas_call(
        functools.partial(_gmlp_prompt_kernel, rows=rows),
        grid=(nrows // rows,),
        in_specs=[
            row(Z_U // GMLP_W), row(Z_V // GMLP_W), par(), par(),
            pl.BlockSpec(w_s.shape, lambda i: (0, 0, 0)),
            pl.BlockSpec(bias_full.shape, lambda i: (0, 0)),
        ],
        out_specs=row(0),
        out_shape=jax.ShapeDtypeStruct((nrows, GMLP_W), BF16),
        compiler_params=_params("parallel"),
        name="gmlp_prompt",
    )(z, z, ln_g.reshape(1, -1), ln_b.reshape(1, -1), w_s, bias_full)


def _gmlp_sample_kernel(u_ref, v_ref, lg_ref, lb_ref, wt_ref, b_ref, o_ref, vr_ref, *, rows, dec):
    vn = _layer_norm(_gelu(v_ref[...]), lg_ref[...], lb_ref[...])
    vr_ref[...] = vn
    v3 = vn.reshape(rows // dec, dec, GMLP_W)
    tpos = lax.broadcasted_iota(jnp.int32, (dec, GMLP_W), 0)
    s3 = jnp.broadcast_to(b_ref[...][None], v3.shape)
    for j in range(dec):
        wj = jnp.where(tpos >= j, wt_ref[j], 0.0)
        s3 = s3 + wj[None] * v3[:, j:j + 1, :]
    u3 = _gelu(u_ref[...]).reshape(rows // dec, dec, GMLP_W)
    o_ref[...] = (u3 * s3).reshape(rows, GMLP_W).astype(o_ref.dtype)


def gmlp_sample(z, ln_g, ln_b, wt, b_dec, row0, nrows, dec):
    rows = _pick(nrows, (256, 128))
    base = row0 // rows
    row = lambda cb, off: pl.BlockSpec((rows, GMLP_W), lambda i: (off + i, cb))
    par = lambda: pl.BlockSpec((1, GMLP_W), lambda i: (0, 0))
    return pl.pallas_call(
        functools.partial(_gmlp_sample_kernel, rows=rows, dec=dec),
        grid=(nrows // rows,),
        in_specs=[
            row(Z_U // GMLP_W, base), row(Z_V // GMLP_W, base), par(), par(),
            pl.BlockSpec(wt.shape, lambda i: (0, 0, 0)),
            pl.BlockSpec(b_dec.shape, lambda i: (0, 0)),
        ],
        out_specs=[row(0, 0), row(0, 0)],
        out_shape=[jax.ShapeDtypeStruct((nrows, GMLP_W), BF16), jax.ShapeDtypeStruct((nrows, GMLP_W), F32)],
        compiler_params=_params("parallel"),
        name="gmlp_sample",
    )(z, z, ln_g.reshape(1, -1), ln_b.reshape(1, -1), wt, b_dec)


def _rope_tables(pos):
    inv = ROPE_THETA ** (-jnp.arange(0, MLA_ROPE, 2, dtype=F32) / MLA_ROPE)
    ang = pos.astype(F32)[:, None] * inv[None, :]
    cos, sin = jnp.cos(ang), jnp.sin(ang)
    zero = jnp.zeros((pos.shape[0], LANES - MLA_ROPE), F32)
    return jnp.concatenate([cos, cos, zero], axis=1), jnp.concatenate([-sin, sin, zero], axis=1)


def _pad_lanes(v, offset=0):
    return jnp.zeros((1, LANES), F32).at[0, offset:offset + v.shape[0]].set(v)


def kernel(x_prompt, x_sample, cache_ckv, cache_krope, state_gdn, state_conv, page_table, norm_mix_g, w_in, mla_q_norm_g, mla_w_uq, mla_qn_g, mla_qr_g, mla_kv_norm_g, mla_kr_g, mla_w_uk, mla_kn_g, mla_w_uv, gdn_conv_w, gdn_a_log, gdn_dt_bias, gdn_o_norm_g, gmlp_ln_g, gmlp_ln_b, gmlp_w_s, gmlp_b_s, w_br_mla, w_br_gdn, w_br_gmlp, w_o, norm_ffn_g, ffn_w_gu, ffn_w_down):
    bp, seq, d = x_prompt.shape
    bs, dec, _ = x_sample.shape
    depth = w_in.shape[0]
    mp, ms = bp * seq, bs * dec
    n_past = page_table.shape[1] * cache_ckv.shape[2]
    assert d == D_MODEL and dec == 8 and mp % ms == 0 and seq % GMLP_CHUNK == 0 and seq % GDN_CHUNK == 0

    x = jnp.concatenate([x_prompt.reshape(mp, d), x_sample.reshape(ms, d)], axis=0)
    pos = jnp.concatenate([jnp.tile(jnp.arange(seq), bp), jnp.tile(n_past + jnp.arange(dec), bs)])
    cs, sg = _rope_tables(pos)
    cache_krope_t = jnp.swapaxes(cache_krope, 2, 3)
    conv0_p = jnp.zeros((1, bp, GDN_CONV - 1, GDN_QKV), F32)
    s0_p = jnp.zeros((1, bp, GDN_HEADS, GDN_DK, GDN_DV), F32)

    o = [0]
    for n in (MLA_Q_RANK, MLA_KV_RANK, MLA_ROPE, GDN_QKV, GDN_W, GDN_HEADS, GDN_HEADS, GMLP_W, GMLP_W, N_BRANCH * D_MODEL):
        o.append(o[-1] + n)
    w_in_t = jnp.swapaxes(w_in, 1, 2)
    w_in_a = jnp.concatenate([
        w_in_t[:, o[0]:o[3]], w_in_t[:, o[5]:o[7]],
        jnp.zeros((depth, ZA_COLS - Z_SEG - MLA_ROPE - 2 * GDN_HEADS, d), F32),
    ], axis=1)
    w_br = (w_br_mla.astype(BF16), w_br_gdn.astype(BF16), w_br_gmlp.astype(BF16))
    w_o_b = w_o.astype(BF16)
    w_down_b = ffn_w_down.astype(BF16)

    outs = [[] for _ in range(9)]
    for l in range(depth):
        w_uq = mla_w_uq[l].reshape(MLA_Q_RANK, MLA_HEADS, MLA_NOPE + MLA_ROPE)
        w_uq = jnp.concatenate([w_uq, jnp.zeros((MLA_Q_RANK, MLA_HEADS, LANES - MLA_ROPE), F32)], axis=-1)
        w_uq = w_uq.reshape(MLA_Q_RANK, -1).astype(BF16)
        w_uk = mla_w_uk[l].reshape(MLA_KV_RANK, -1).astype(BF16)
        w_uv = mla_w_uv[l].reshape(MLA_KV_RANK, -1).astype(BF16)
        w_ukt = jnp.transpose(mla_w_uk[l], (1, 2, 0)).astype(BF16)
        w_uv_h = jnp.transpose(mla_w_uv[l], (1, 0, 2)).astype(BF16)
        g_kr_pad = _pad_lanes(mla_kr_g[l])
        g_qr_pad = _pad_lanes(mla_qr_g[l])
        a_log_row = _pad_lanes(gdn_a_log[l], SEG_A)
        dt_row = _pad_lanes(gdn_dt_bias[l], SEG_A)
        bias_full = jnp.repeat(gmlp_b_s[l].T, GMLP_GROUP_W, axis=1)
        wt_dec = jnp.repeat(jnp.transpose(gmlp_w_s[l][:, :dec, :dec], (2, 1, 0)), GMLP_GROUP_W, axis=2)

        z_a = norm_matmul_t(x, norm_mix_g[l], w_in_a, l, 0, ZA_COLS, name="in_proj_a")
        z_b = norm_matmul_t(x, norm_mix_g[l], w_in_t, l, o[3], o[5] - o[3], name="in_proj_b")
        z_c = norm_matmul_t(x, norm_mix_g[l], w_in_t, l, o[7], o[10] - o[7], name="in_proj_c")

        q_lat, ckv, kr_pad = mla_prep(z_a, cs, sg, mla_q_norm_g[l], mla_kv_norm_g[l], g_kr_pad)
        qcat = q_project(q_lat, w_uq, cs, sg, mla_qn_g[l], g_qr_pad)
        kcat, vexp = kv_expand(ckv, kr_pad, w_uk, w_uv, mla_kn_g[l], mp)
        o_mla_p = mla_prompt_attention(qcat, kcat, vexp, bp, seq)
        q_abs = absorb_queries(qcat, w_ukt, mla_kn_g[l], mp, ms)
        q_abs = jnp.transpose(q_abs.reshape(MLA_HEADS, bs, dec, MLA_KV_RANK), (1, 0, 2, 3)).reshape(bs, MLA_HEADS * dec, MLA_KV_RANK)
        q_rope = qcat[mp:].reshape(bs, dec, MLA_HEADS, MLA_NOPE + LANES)[..., MLA_NOPE:MLA_NOPE + MLA_ROPE]
        q_rope = jnp.transpose(q_rope, (0, 2, 1, 3)).reshape(bs, MLA_HEADS * dec, MLA_ROPE).astype(F32)
        o_lat = mla_sample_attention(q_abs, q_rope, ckv, kr_pad, w_ukt.reshape(-1, MLA_KV_RANK), cache_ckv, cache_krope_t, page_table, l, mp, dec)
        o_lat = jnp.transpose(o_lat.reshape(bs, MLA_HEADS, dec, MLA_KV_RANK), (1, 0, 2, 3)).reshape(MLA_HEADS, ms, MLA_KV_RANK)
        o_mla_s = unabsorb_values(o_lat, w_uv_h)

        o_gdn_p, conv_p, s_p = gdn_branch(z_a, z_b, conv0_p, s0_p, 0, gdn_conv_w[l], a_log_row, dt_row, gdn_o_norm_g[l], 0, bp, seq)
        o_gdn_s, conv_s, s_s = gdn_branch(z_a, z_b, state_conv, state_gdn, l, gdn_conv_w[l], a_log_row, dt_row, gdn_o_norm_g[l], mp, bs, dec)

        o_gmlp_p = gmlp_prompt(z_c, gmlp_ln_g[l], gmlp_ln_b[l], gmlp_w_s[l], bias_full, mp)
        o_gmlp_s, v_rows = gmlp_sample(z_c, gmlp_ln_g[l], gmlp_ln_b[l], wt_dec, bias_full[:dec], mp, ms, dec)

        merged = gated_merge((o_mla_p, o_gdn_p, o_gmlp_p), (o_mla_s, o_gdn_s, o_gmlp_s), w_br, l, z_c)
        x = matmul_residual(merged, w_o_b, l, x, (1024, 512, 256, 128), name="out_proj")
        act = swiglu_up(x, norm_ffn_g[l], ffn_w_gu, l)
        x = matmul_residual(act, w_down_b, l, x, (1024, 512, 256, 128), name="ffn_down")

        kr = kr_pad[:, :MLA_ROPE]
        for i, v in enumerate((
            ckv[:mp].reshape(bp, seq, -1), kr[:mp].reshape(bp, seq, -1), s_p, conv_p,
            ckv[mp:].reshape(bs, dec, -1), kr[mp:].reshape(bs, dec, -1), s_s, conv_s, v_rows.reshape(bs, dec, -1),
        )):
            outs[i].append(v)

    return (x[:mp].reshape(bp, seq, d), x[mp:].reshape(bs, dec, d)) + tuple(jnp.stack(v, axis=0) for v in outs)
```
